```python
import jax, jax.numpy as jnp
from jax import lax
import numpy as np

D_MODEL = 2048
BATCH = 4
SEQ = 2048
DEPTH = 1
DEC_BATCH = 128
DEC_SEQ = 1
PAST_LEN = 16384
PAGE_SIZE = 128

MIX_DIM = D_MODEL
CONV_DIM = MIX_DIM // 2
POOL_DIM = MIX_DIM - CONV_DIM
POOL_WINDOWS = (2, 4, 8, 16)
N_POOL_GROUPS = len(POOL_WINDOWS)
POOL_GROUP_DIM = POOL_DIM // N_POOL_GROUPS
POOL_CTX = max(POOL_WINDOWS) - 1
CONV_WIDTH = 31
CONV_CTX = CONV_WIDTH - 1
PROJ_IN = 2 * CONV_DIM + POOL_DIM
D_FF = ((8 * D_MODEL // 3 + 255) // 256) * 256
PLE_DIM = 256
EPS = 1e-6

kernel_name = "hymba_conv_pool_macaron_decoder_step"


def _rmsnorm(x, g):
    xf = x.astype(jnp.float32)
    xf = xf * lax.rsqrt(jnp.mean(xf * xf, axis=-1, keepdims=True) + EPS)
    return (xf * g.astype(jnp.float32)).astype(x.dtype)


def _layernorm(x, g, b):
    xf = x.astype(jnp.float32)
    mu = jnp.mean(xf, axis=-1, keepdims=True)
    var = jnp.mean(jnp.square(xf - mu), axis=-1, keepdims=True)
    y = (xf - mu) * lax.rsqrt(var + EPS) * g.astype(jnp.float32) + b.astype(jnp.float32)
    return y.astype(x.dtype)


def _swiglu(h, w_in, w_out):
    gu = h @ w_in
    gate, up = gu[..., :D_FF], gu[..., D_FF:]
    return (jax.nn.silu(gate) * up) @ w_out


def _conv_group(a_val, a_gate, conv_buf, conv_w, conv_b, conv_ln_g, conv_ln_b):
    v = a_val * jax.nn.sigmoid(a_gate)
    v_ext = jnp.concatenate([conv_buf.astype(v.dtype), v], axis=1)
    out = lax.conv_general_dilated(
        v_ext, conv_w[:, None, :].astype(v.dtype), window_strides=(1,), padding="VALID",
        dimension_numbers=("NWC", "WIO", "NWC"), feature_group_count=CONV_DIM)
    out = out + conv_b
    out = jax.nn.silu(_layernorm(out, conv_ln_g, conv_ln_b))
    return out, v_ext[:, -CONV_CTX:]


def _pool_group(u, pool_buf, pos0, pool_w, pool_scale):
    B, T, _ = u.shape
    u_ext = jnp.concatenate([pool_buf.astype(u.dtype), u], axis=1)
    c = jnp.cumsum(u_ext.astype(jnp.float32), axis=1)
    c = jnp.pad(c, ((0, 0), (1, 0), (0, 0)))
    pos = pos0 + jnp.arange(T, dtype=jnp.int32)
    uf = u.astype(jnp.float32)
    outs = []
    for g, w in enumerate(POOL_WINDOWS):
        sl = slice(g * POOL_GROUP_DIM, (g + 1) * POOL_GROUP_DIM)
        s = c[:, 1 + POOL_CTX:1 + POOL_CTX + T, sl] - c[:, 1 + POOL_CTX - w:1 + POOL_CTX - w + T, sl]
        cnt = jnp.minimum(pos + 1, w).astype(jnp.float32)[None, :, None]
        outs.append(s / cnt - uf[..., sl])
    d = jnp.stack(outs, axis=2).astype(u.dtype)
    y = jnp.einsum("btgc,gcd->btgd", d, pool_w).reshape(B, T, POOL_DIM)
    return y * pool_scale, u_ext[:, -POOL_CTX:]


def _layer(x, p, conv_buf, pool_buf, pos0,
           norm_ffn1, w_ffn1_in, w_ffn1_out, norm_mix, w_in, conv_w, conv_b, conv_ln_g, conv_ln_b,
           pool_w, pool_scale, w_out, norm_ffn2, w_ffn2_in, w_ffn2_out, norm_ple, w_ple_gate, w_ple_proj):
    x = x + 0.5 * _swiglu(_rmsnorm(x, norm_ffn1), w_ffn1_in, w_ffn1_out)
    h = _rmsnorm(x, norm_mix)
    proj = h @ w_in
    a_val = proj[..., :CONV_DIM]
    a_gate = proj[..., CONV_DIM:2 * CONV_DIM]
    u_pool = proj[..., 2 * CONV_DIM:]
    conv_out, new_conv = _conv_group(a_val, a_gate, conv_buf, conv_w, conv_b, conv_ln_g, conv_ln_b)
    pool_out, new_pool = _pool_group(u_pool, pool_buf, pos0, pool_w, pool_scale)
    x = x + jnp.concatenate([conv_out, pool_out], axis=-1) @ w_out
    x = x + 0.5 * _swiglu(_rmsnorm(x, norm_ffn2), w_ffn2_in, w_ffn2_out)
    gate = jax.nn.sigmoid(_rmsnorm(x, norm_ple) @ w_ple_gate)
    x = x + gate * (p @ w_ple_proj)
    return x, new_conv, new_pool


def setup_inputs(seed: int = 0) -> dict:
    key = jax.random.key(seed)
    ks = jax.random.split(key, 32)
    f32 = jnp.float32

    def nrm(k, shape, scale):
        return jax.random.normal(k, shape, f32) * scale

    def gain(k, shape):
        return 1.0 + 0.02 * jax.random.normal(k, shape, f32)

    return {
        "x_prompt": nrm(ks[0], (BATCH, SEQ, D_MODEL), 1.0),
        "x_sample": nrm(ks[1], (DEC_BATCH, DEC_SEQ, D_MODEL), 1.0),
        "state_conv": nrm(ks[2], (DEPTH, DEC_BATCH, CONV_CTX, CONV_DIM), 0.5),
        "state_pool": nrm(ks[3], (DEPTH, DEC_BATCH, POOL_CTX, POOL_DIM), 1.0),
        "p_prompt": nrm(ks[4], (DEPTH, BATCH, SEQ, PLE_DIM), 1.0),
        "p_sample": nrm(ks[5], (DEPTH, DEC_BATCH, DEC_SEQ, PLE_DIM), 1.0),
        "norm_ffn1": gain(ks[6], (DEPTH, D_MODEL)),
        "w_ffn1_in": nrm(ks[7], (DEPTH, D_MODEL, 2 * D_FF), D_MODEL ** -0.5),
        "w_ffn1_out": nrm(ks[8], (DEPTH, D_FF, D_MODEL), D_FF ** -0.5),
        "norm_mix": gain(ks[9], (DEPTH, D_MODEL)),
        "w_in": nrm(ks[10], (DEPTH, D_MODEL, PROJ_IN), D_MODEL ** -0.5),
        "conv_w": nrm(ks[11], (DEPTH, CONV_WIDTH, CONV_DIM), CONV_WIDTH ** -0.5),
        "conv_b": nrm(ks[12], (DEPTH, CONV_DIM), 0.02),
        "conv_ln_g": gain(ks[13], (DEPTH, CONV_DIM)),
        "conv_ln_b": nrm(ks[14], (DEPTH, CONV_DIM), 0.02),
        "pool_w": nrm(ks[15], (DEPTH, N_POOL_GROUPS, POOL_GROUP_DIM, POOL_GROUP_DIM), POOL_GROUP_DIM ** -0.5),
        "pool_scale": gain(ks[16], (DEPTH, POOL_DIM)),
        "w_out": nrm(ks[17], (DEPTH, MIX_DIM, D_MODEL), MIX_DIM ** -0.5),
        "norm_ffn2": gain(ks[18], (DEPTH, D_MODEL)),
        "w_ffn2_in": nrm(ks[19], (DEPTH, D_MODEL, 2 * D_FF), D_MODEL ** -0.5),
        "w_ffn2_out": nrm(ks[20], (DEPTH, D_FF, D_MODEL), D_FF ** -0.5),
        "norm_ple": gain(ks[21], (DEPTH, D_MODEL)),
        "w_ple_gate": nrm(ks[22], (DEPTH, D_MODEL, D_MODEL), D_MODEL ** -0.5),
        "w_ple_proj": nrm(ks[23], (DEPTH, PLE_DIM, D_MODEL), PLE_DIM ** -0.5),
        "norm_final": gain(ks[24], (D_MODEL,)),
    }


def reference(x_prompt, x_sample, state_conv, state_pool, p_prompt, p_sample,
              norm_ffn1, w_ffn1_in, w_ffn1_out, norm_mix, w_in, conv_w, conv_b, conv_ln_g, conv_ln_b,
              pool_w, pool_scale, w_out, norm_ffn2, w_ffn2_in, w_ffn2_out, norm_ple, w_ple_gate,
              w_ple_proj, norm_final):
    xp, xs = x_prompt, x_sample
    conv_p_list, conv_s_list, pool_p_list, pool_s_list = [], [], [], []
    for i in range(DEPTH):
        weights = (norm_ffn1[i], w_ffn1_in[i], w_ffn1_out[i], norm_mix[i], w_in[i], conv_w[i], conv_b[i],
                   conv_ln_g[i], conv_ln_b[i], pool_w[i], pool_scale[i], w_out[i], norm_ffn2[i],
                   w_ffn2_in[i], w_ffn2_out[i], norm_ple[i], w_ple_gate[i], w_ple_proj[i])
        zero_conv = jnp.zeros((xp.shape[0], CONV_CTX, CONV_DIM), xp.dtype)
        zero_pool = jnp.zeros((xp.shape[0], POOL_CTX, POOL_DIM), xp.dtype)
        xp, cp, pp = _layer(xp, p_prompt[i], zero_conv, zero_pool, 0, *weights)
        xs, cs, ps = _layer(xs, p_sample[i], state_conv[i], state_pool[i], PAST_LEN, *weights)
        conv_p_list.append(cp)
        conv_s_list.append(cs)
        pool_p_list.append(pp)
        pool_s_list.append(ps)
    y_prompt = _rmsnorm(xp, norm_final)
    y_sample = _rmsnorm(xs, norm_final)
    new_conv_prompt = jnp.stack(conv_p_list, axis=0)
    new_conv_sample = jnp.stack(conv_s_list, axis=0)
    new_pool_prompt = jnp.stack(pool_p_list, axis=0)
    new_pool_sample = jnp.stack(pool_s_list, axis=0)
    return (y_prompt, y_sample, new_conv_prompt, new_conv_sample, new_pool_prompt, new_pool_sample)
```

```python
import functools

import jax
import jax.numpy as jnp
from jax import lax
from jax.experimental import pallas as pl
from jax.experimental.pallas import tpu as pltpu

D_MODEL = 2048
D_FF = 5632
CONV_DIM = 1024
POOL_DIM = 1024
POOL_WINDOWS = (2, 4, 8, 16)
POOL_GROUP_DIM = POOL_DIM // len(POOL_WINDOWS)
POOL_CTX = max(POOL_WINDOWS) - 1
CONV_WIDTH = 31
CONV_CTX = CONV_WIDTH - 1
PLE_DIM = 256
PAST_LEN = 16384
EPS = 1e-6

F32 = jnp.float32
BF16 = jnp.bfloat16

VMEM_LIMIT_BYTES = 56 * 1024 * 1024
PROMPT_TM = 512
FFN_TF = 512
CONV_HALO = 32
POOL_HALO = 16
ROW_CHUNK = 32
LANE_GROUP = 256
SUBLANES = 8


def _rms(x, g):
    ms = jnp.mean(x * x, axis=-1, keepdims=True)
    return x * lax.rsqrt(ms + EPS) * g


def _params(sem):
    return pltpu.CompilerParams(dimension_semantics=sem, vmem_limit_bytes=VMEM_LIMIT_BYTES)


def _resident(shape):
    nd = len(shape)
    return pl.BlockSpec(shape, lambda *_: (0,) * nd, pipeline_mode=pl.Buffered(1))


def _ffn_kernel(x_ref, g_ref, wg_ref, wu_ref, wo_ref, o_ref, h_ref):
    @pl.when(pl.program_id(1) == 0)
    def _():
        x = x_ref[...]
        h_ref[...] = _rms(x, g_ref[...]).astype(BF16)
        o_ref[...] = x

    h = h_ref[...]
    gate = jnp.dot(h, wg_ref[...], preferred_element_type=F32)
    up = jnp.dot(h, wu_ref[...], preferred_element_type=F32)
    act = (gate * jax.nn.sigmoid(gate) * up * 0.5).astype(BF16)
    o_ref[...] += jnp.dot(act, wo_ref[...], preferred_element_type=F32)


def _ffn(x, g, w_in, w_out, *, tm, tf):
    rows = x.shape[0]
    nf = D_FF // tf
    return pl.pallas_call(
        _ffn_kernel,
        out_shape=jax.ShapeDtypeStruct((rows, D_MODEL), F32),
        grid=(rows // tm, nf),
        in_specs=[
            pl.BlockSpec((tm, D_MODEL), lambda i, j: (i, 0)),
            pl.BlockSpec((1, D_MODEL), lambda i, j: (0, 0)),
            pl.BlockSpec((D_MODEL, tf), lambda i, j: (0, j)),
            pl.BlockSpec((D_MODEL, tf), lambda i, j: (0, j + nf)),
            pl.BlockSpec((tf, D_MODEL), lambda i, j: (j, 0)),
        ],
        out_specs=pl.BlockSpec((tm, D_MODEL), lambda i, j: (i, 0)),
        scratch_shapes=[pltpu.VMEM((tm, D_MODEL), BF16)],
        compiler_params=_params(("parallel", "arbitrary")),
        name="ffn",
    )(x, g, w_in, w_in, w_out)


def _mix_in_kernel(x_ref, g_ref, w_ref, v_ref, u_ref):
    h = _rms(x_ref[...], g_ref[...]).astype(BF16)
    nc = 512
    for c in range(CONV_DIM // nc):
        a_val = jnp.dot(h, w_ref[:, c * nc:(c + 1) * nc], preferred_element_type=F32)
        a_gate = jnp.dot(h, w_ref[:, CONV_DIM + c * nc:CONV_DIM + (c + 1) * nc],
                         preferred_element_type=F32)
        v_ref[:, c * nc:(c + 1) * nc] = a_val * jax.nn.sigmoid(a_gate)
        u_ref[:, c * nc:(c + 1) * nc] = jnp.dot(
            h, w_ref[:, 2 * CONV_DIM + c * nc:2 * CONV_DIM + (c + 1) * nc],
            preferred_element_type=F32)


def _mix_in(x, g, w_in, *, tm):
    rows = x.shape[0]
    return pl.pallas_call(
        _mix_in_kernel,
        out_shape=(jax.ShapeDtypeStruct((rows, CONV_DIM), F32),
                   jax.ShapeDtypeStruct((rows, POOL_DIM), F32)),
        grid=(rows // tm,),
        in_specs=[
            pl.BlockSpec((tm, D_MODEL), lambda i: (i, 0)),
            _resident((1, D_MODEL)),
            _resident(w_in.shape),
        ],
        out_specs=(pl.BlockSpec((tm, CONV_DIM), lambda i: (i, 0)),
                   pl.BlockSpec((tm, POOL_DIM), lambda i: (i, 0))),
        compiler_params=_params(("parallel",)),
        name="mix_in",
    )(x, g, w_in)


def _conv_post(acc, cb, lg, lb):
    y = acc + cb
    mu = jnp.mean(y, axis=-1, keepdims=True)
    yc = y - mu
    var = jnp.mean(yc * yc, axis=-1, keepdims=True)
    z = yc * lax.rsqrt(var + EPS) * lg + lb
    return z * jax.nn.sigmoid(z)


def _mix_tail(a_ref, d_ref, x_ref, pw_ref, ps_ref, wo_ref, o_ref):
    for g in range(len(POOL_WINDOWS)):
        sl = slice(g * POOL_GROUP_DIM, (g + 1) * POOL_GROUP_DIM)
        y = jnp.dot(d_ref[:, sl], pw_ref[g], preferred_element_type=F32) * ps_ref[:, sl]
        a_ref[:, CONV_DIM + g * POOL_GROUP_DIM:CONV_DIM + (g + 1) * POOL_GROUP_DIM] = y.astype(BF16)
    o_ref[...] = x_ref[...] + jnp.dot(a_ref[...], wo_ref[...], preferred_element_type=F32)


def _shifted_rows(win, n_out, offsets):
    n = win.shape[0]
    for b in range(SUBLANES):
        group = [o for o in offsets if o % SUBLANES == b]
        if not group:
            continue
        rolled = win if b == 0 else pltpu.roll(win, n - b, axis=0)
        for o in group:
            yield o, rolled[o - b:o - b + n_out]


def _mix_out_prompt_kernel(v_ref, vh_ref, u_ref, uh_ref, x_ref, cw_ref, cb_ref, lg_ref, lb_ref,
                           pw_ref, ps_ref, wo_ref, o_ref, vext_ref, uext_ref, a_ref, d_ref, *, tm):
    t = pl.program_id(1)
    first = t == 0

    vext_ref[0:CONV_HALO, :] = jnp.where(first, 0.0, vh_ref[...])
    vext_ref[CONV_HALO:, :] = v_ref[...]
    uext_ref[0:POOL_HALO, :] = jnp.where(first, 0.0, uh_ref[...])
    uext_ref[POOL_HALO:, :] = u_ref[...]

    cb, lg, lb = cb_ref[...], lg_ref[...], lb_ref[...]
    pos0 = t * tm

    def chunk(r, carry):
        r0 = pl.multiple_of(r * ROW_CHUNK, ROW_CHUNK)
        accs = []
        for c in range(CONV_DIM // LANE_GROUP):
            sl = slice(c * LANE_GROUP, (c + 1) * LANE_GROUP)
            win = vext_ref[pl.ds(r0, ROW_CHUNK + CONV_HALO), sl]
            offsets = [CONV_HALO - CONV_CTX + k for k in range(CONV_WIDTH)]
            acc = jnp.zeros((ROW_CHUNK, LANE_GROUP), F32)
            for o, rows in _shifted_rows(win, ROW_CHUNK, offsets):
                k = o - (CONV_HALO - CONV_CTX)
                acc = acc + cw_ref[k:k + 1, sl] * rows
            accs.append(acc)
        conv = _conv_post(jnp.concatenate(accs, axis=1), cb, lg, lb)
        a_ref[pl.ds(r0, ROW_CHUNK), 0:CONV_DIM] = conv.astype(BF16)

        pos = pos0 + r0 + lax.broadcasted_iota(jnp.int32, (ROW_CHUNK, 1), 0)
        for g, w in enumerate(POOL_WINDOWS):
            sl = slice(g * POOL_GROUP_DIM, (g + 1) * POOL_GROUP_DIM)
            win = uext_ref[pl.ds(r0, ROW_CHUNK + POOL_HALO), sl]
            shifted = dict(_shifted_rows(win, ROW_CHUNK, [POOL_HALO - i for i in range(w)]))
            cur = shifted[POOL_HALO]
            s = cur
            for i in range(1, w):
                s = s + shifted[POOL_HALO - i]
            cnt = jnp.minimum(pos + 1, w).astype(F32)
            d_ref[pl.ds(r0, ROW_CHUNK), sl] = (s / cnt - cur).astype(BF16)
        return carry

    lax.fori_loop(0, tm // ROW_CHUNK, chunk, 0)
    _mix_tail(a_ref, d_ref, x_ref, pw_ref, ps_ref, wo_ref, o_ref)


def _mix_out_prompt(v, u, x, cw, cb, lg, lb, pw, ps, wo, *, batch, seq, tm):
    nt = seq // tm
    vh_per_tile, uh_per_tile = tm // CONV_HALO, tm // POOL_HALO

    def row(b, t):
        return (b * nt + t, 0)

    def vhalo(b, t):
        return (jnp.maximum((b * nt + t) * vh_per_tile - 1, 0), 0)

    def uhalo(b, t):
        return (jnp.maximum((b * nt + t) * uh_per_tile - 1, 0), 0)

    return pl.pallas_call(
        functools.partial(_mix_out_prompt_kernel, tm=tm),
        out_shape=jax.ShapeDtypeStruct((batch * seq, D_MODEL), F32),
        grid=(batch, nt),
        in_specs=[
            pl.BlockSpec((tm, CONV_DIM), row),
            pl.BlockSpec((CONV_HALO, CONV_DIM), vhalo),
            pl.BlockSpec((tm, POOL_DIM), row),
            pl.BlockSpec((POOL_HALO, POOL_DIM), uhalo),
            pl.BlockSpec((tm, D_MODEL), row),
            _resident(cw.shape), _resident(cb.shape), _resident(lg.shape), _resident(lb.shape),
            _resident(pw.shape), _resident(ps.shape), _resident(wo.shape),
        ],
        out_specs=pl.BlockSpec((tm, D_MODEL), row),
        scratch_shapes=[
            pltpu.VMEM((CONV_HALO + tm, CONV_DIM), F32),
            pltpu.VMEM((POOL_HALO + tm, POOL_DIM), F32),
            pltpu.VMEM((tm, D_MODEL), BF16),
            pltpu.VMEM((tm, POOL_DIM), BF16),
        ],
        compiler_params=_params(("parallel", "arbitrary")),
        name="mix_out_prompt",
    )(v, v, u, u, x, cw, cb, lg, lb, pw, ps, wo)


def _mix_out_sample_kernel(v_ref, u_ref, sc_ref, sp_ref, x_ref, cw_ref, cb_ref, lg_ref, lb_ref,
                           pw_ref, ps_ref, wo_ref, o_ref, acc_ref, s_ref, a_ref, d_ref, *, nb):
    cw_ctx = cw_ref[0:CONV_CTX, :]
    rowi = lax.broadcasted_iota(jnp.int32, (POOL_CTX, POOL_DIM), 0)
    lane = lax.broadcasted_iota(jnp.int32, (POOL_CTX, POOL_DIM), 1)
    win = jnp.zeros((POOL_CTX, POOL_DIM), jnp.int32)
    for g, w in enumerate(POOL_WINDOWS):
        win = jnp.where(lane // POOL_GROUP_DIM == g, w, win)
    in_window = (rowi >= POOL_CTX + 1 - win).astype(F32)

    def per_seq(b, carry):
        acc_ref[pl.ds(b, 1), :] = jnp.sum(sc_ref[b] * cw_ctx, axis=0, keepdims=True)
        s_ref[pl.ds(b, 1), :] = jnp.sum(sp_ref[b] * in_window, axis=0, keepdims=True)
        return carry

    lax.fori_loop(0, nb, per_seq, 0)

    acc = acc_ref[...] + cw_ref[CONV_CTX:CONV_WIDTH, :] * v_ref[...]
    a_ref[:, 0:CONV_DIM] = _conv_post(acc, cb_ref[...], lg_ref[...], lb_ref[...]).astype(BF16)

    u = u_ref[...]
    s = s_ref[...] + u
    lane1 = lax.broadcasted_iota(jnp.int32, (1, POOL_DIM), 1)
    cnt = jnp.zeros((1, POOL_DIM), F32)
    for g, w in enumerate(POOL_WINDOWS):
        cnt = jnp.where(lane1 // POOL_GROUP_DIM == g, float(min(PAST_LEN + 1, w)), cnt)
    d_ref[...] = (s / cnt - u).astype(BF16)
    _mix_tail(a_ref, d_ref, x_ref, pw_ref, ps_ref, wo_ref, o_ref)


def _mix_out_sample(v, u, sc, sp, x, cw, cb, lg, lb, pw, ps, wo):
    nb = v.shape[0]
    args = (v, u, sc, sp, x, cw, cb, lg, lb, pw, ps, wo)
    return pl.pallas_call(
        functools.partial(_mix_out_sample_kernel, nb=nb),
        out_shape=jax.ShapeDtypeStruct((nb, D_MODEL), F32),
        grid=(1,),
        in_specs=[_resident(a.shape) for a in args],
        out_specs=pl.BlockSpec((nb, D_MODEL), lambda i: (0, 0)),
        scratch_shapes=[
            pltpu.VMEM((nb, CONV_DIM), F32),
            pltpu.VMEM((nb, POOL_DIM), F32),
            pltpu.VMEM((nb, D_MODEL), BF16),
            pltpu.VMEM((nb, POOL_DIM), BF16),
        ],
        compiler_params=_params(("arbitrary",)),
        name="mix_out_sample",
    )(*args)


def _ple_kernel(x_ref, p_ref, gp_ref, wg_ref, wp_ref, gf_ref, o_ref):
    x = x_ref[...]
    r = _rms(x, gp_ref[...]).astype(BF16)
    p = p_ref[...].astype(BF16)
    nc = 512
    for c in range(D_MODEL // nc):
        sl = slice(c * nc, (c + 1) * nc)
        gate = jax.nn.sigmoid(jnp.dot(r, wg_ref[:, sl], preferred_element_type=F32))
        proj = jnp.dot(p, wp_ref[:, sl], preferred_element_type=F32)
        o_ref[:, sl] = x_ref[:, sl] + gate * proj
    o_ref[...] = _rms(o_ref[...], gf_ref[...])


def _ple(x, p, gp, wg, wp, gf, *, tm):
    rows = x.shape[0]
    return pl.pallas_call(
        _ple_kernel,
        out_shape=jax.ShapeDtypeStruct((rows, D_MODEL), F32),
        grid=(rows // tm,),
        in_specs=[
            pl.BlockSpec((tm, D_MODEL), lambda i: (i, 0)),
            pl.BlockSpec((tm, PLE_DIM), lambda i: (i, 0)),
            _resident((1, D_MODEL)), _resident(wg.shape), _resident(wp.shape), _resident((1, D_MODEL)),
        ],
        out_specs=pl.BlockSpec((tm, D_MODEL), lambda i: (i, 0)),
        compiler_params=_params(("parallel",)),
        name="ple",
    )(x, p, gp, wg, wp, gf)


def kernel(x_prompt, x_sample, state_conv, state_pool, p_prompt, p_sample, norm_ffn1, w_ffn1_in, w_ffn1_out, norm_mix, w_in, conv_w, conv_b, conv_ln_g, conv_ln_b, pool_w, pool_scale, w_out, norm_ffn2, w_ffn2_in, w_ffn2_out, norm_ple, w_ple_gate, w_ple_proj, norm_final):
    assert norm_ffn1.shape[0] == 1, "the final norm is fused into the layer's last stage: one layer only"
    batch, seq, _ = x_prompt.shape
    nb = x_sample.shape[0]
    xp = x_prompt.reshape(batch * seq, D_MODEL)
    xs = x_sample.reshape(nb, D_MODEL)
    row = lambda a: a.reshape(1, -1)

    w1i, w1o = w_ffn1_in[0].astype(BF16), w_ffn1_out[0].astype(BF16)
    w2i, w2o = w_ffn2_in[0].astype(BF16), w_ffn2_out[0].astype(BF16)
    wi, wo = w_in[0].astype(BF16), w_out[0].astype(BF16)
    pw, wg, wp = pool_w[0].astype(BF16), w_ple_gate[0].astype(BF16), w_ple_proj[0].astype(BF16)
    g1, gm, g2, gp, gf = row(norm_ffn1[0]), row(norm_mix[0]), row(norm_ffn2[0]), row(norm_ple[0]), row(norm_final)
    mix = (conv_w[0], row(conv_b[0]), row(conv_ln_g[0]), row(conv_ln_b[0]), pw, row(pool_scale[0]), wo)

    x1 = _ffn(xp, g1, w1i, w1o, tm=PROMPT_TM, tf=FFN_TF)
    v, u = _mix_in(x1, gm, wi, tm=PROMPT_TM)
    x2 = _mix_out_prompt(v, u, x1, *mix, batch=batch, seq=seq, tm=PROMPT_TM)
    x3 = _ffn(x2, g2, w2i, w2o, tm=PROMPT_TM, tf=FFN_TF)
    yp = _ple(x3, p_prompt[0].reshape(batch * seq, PLE_DIM), gp, wg, wp, gf, tm=PROMPT_TM)
    new_conv_p = v.reshape(batch, seq, CONV_DIM)[:, seq - CONV_CTX:]
    new_pool_p = u.reshape(batch, seq, POOL_DIM)[:, seq - POOL_CTX:]

    x1s = _ffn(xs, g1, w1i, w1o, tm=nb, tf=FFN_TF)
    vs, us = _mix_in(x1s, gm, wi, tm=nb)
    x2s = _mix_out_sample(vs, us, state_conv[0], state_pool[0], x1s, *mix)
    x3s = _ffn(x2s, g2, w2i, w2o, tm=nb, tf=FFN_TF)
    ys = _ple(x3s, p_sample[0].reshape(nb, PLE_DIM), gp, wg, wp, gf, tm=nb)
    new_conv_s = jnp.concatenate([state_conv[0][:, 1:], vs[:, None, :]], axis=1)
    new_pool_s = jnp.concatenate([state_pool[0][:, 1:], us[:, None, :]], axis=1)

    return (yp.reshape(batch, seq, D_MODEL), ys.reshape(nb, 1, D_MODEL),
            new_conv_p[None], new_conv_s[None], new_pool_p[None], new_pool_s[None])
```

```python
import functools

import jax
import jax.numpy as jnp
from jax import lax
from jax.experimental import pallas as pl
from jax.experimental.pallas import tpu as pltpu

D_MODEL = 2048
D_FF = 5632
CONV_DIM = 1024
POOL_DIM = 1024
POOL_WINDOWS = (2, 4, 8, 16)
POOL_GROUP_DIM = POOL_DIM // len(POOL_WINDOWS)
POOL_CTX = max(POOL_WINDOWS) - 1
CONV_WIDTH = 31
CONV_CTX = CONV_WIDTH - 1
PLE_DIM = 256
PAST_LEN = 16384
EPS = 1e-6

F32 = jnp.float32
BF16 = jnp.bfloat16

VMEM_LIMIT_BYTES = 62 * 1024 * 1024
PROMPT_TM = 512
FFN_TM = 1024
FFN_TF = 512
CONV_HALO = 32
POOL_HALO = 16
ROW_CHUNK = 32
LANE_GROUP = 256
SUBLANES = 8


def _rms(x, g):
    ms = jnp.mean(x * x, axis=-1, keepdims=True)
    return x * lax.rsqrt(ms + EPS) * g


def _params(sem):
    return pltpu.CompilerParams(dimension_semantics=sem, vmem_limit_bytes=VMEM_LIMIT_BYTES)


def _resident(shape):
    nd = len(shape)
    return pl.BlockSpec(shape, lambda *_: (0,) * nd, pipeline_mode=pl.Buffered(1))


def _ffn_rows(j, x_ref, g_ref, wg_ref, wu_ref, wo_ref, o_ref, h_ref):
    @pl.when(j == 0)
    def _():
        x = x_ref[...]
        h_ref[...] = _rms(x, g_ref[...]).astype(BF16)
        o_ref[...] = x

    h = h_ref[...]
    gate = jnp.dot(h, wg_ref[...].astype(BF16), preferred_element_type=F32)
    up = jnp.dot(h, wu_ref[...].astype(BF16), preferred_element_type=F32)
    act = (gate * jax.nn.sigmoid(gate) * up * 0.5).astype(BF16)
    o_ref[...] += jnp.dot(act, wo_ref[...].astype(BF16), preferred_element_type=F32)


def _ffn_kernel(xp_ref, xs_ref, g_ref, wg_ref, wu_ref, wo_ref, op_ref, os_ref, h_ref, *, n_prompt_tiles):
    i, j = pl.program_id(0), pl.program_id(1)

    @pl.when(i < n_prompt_tiles)
    def _():
        _ffn_rows(j, xp_ref, g_ref, wg_ref, wu_ref, wo_ref, op_ref, h_ref)

    @pl.when(i == n_prompt_tiles)
    def _():
        _ffn_rows(j, xs_ref, g_ref, wg_ref, wu_ref, wo_ref, os_ref, h_ref.at[0:xs_ref.shape[0]])


def _ffn(xp, xs, g, w_in, w_out, *, tm, tf):
    n_prompt_tiles = xp.shape[0] // tm
    ns = xs.shape[0]
    nf = D_FF // tf
    prompt_tile = lambda i, j: (jnp.minimum(i, n_prompt_tiles - 1), 0)
    return pl.pallas_call(
        functools.partial(_ffn_kernel, n_prompt_tiles=n_prompt_tiles),
        out_shape=(jax.ShapeDtypeStruct(xp.shape, F32), jax.ShapeDtypeStruct(xs.shape, F32)),
        grid=(n_prompt_tiles + 1, nf),
        in_specs=[
            pl.BlockSpec((tm, D_MODEL), prompt_tile, pipeline_mode=pl.Buffered(1)),
            pl.BlockSpec((ns, D_MODEL), lambda i, j: (0, 0), pipeline_mode=pl.Buffered(1)),
            pl.BlockSpec((1, D_MODEL), lambda i, j: (0, 0), pipeline_mode=pl.Buffered(1)),
            pl.BlockSpec((D_MODEL, tf), lambda i, j: (0, j)),
            pl.BlockSpec((D_MODEL, tf), lambda i, j: (0, j + nf)),
            pl.BlockSpec((tf, D_MODEL), lambda i, j: (j, 0)),
        ],
        out_specs=(pl.BlockSpec((tm, D_MODEL), prompt_tile),
                   pl.BlockSpec((ns, D_MODEL), lambda i, j: (0, 0))),
        scratch_shapes=[pltpu.VMEM((tm, D_MODEL), BF16)],
        compiler_params=_params(("arbitrary", "arbitrary")),
        name="ffn",
    )(xp, xs, g, w_in, w_in, w_out)


def _mix_in_kernel(x_ref, g_ref, w_ref, v_ref, u_ref):
    h = _rms(x_ref[...], g_ref[...]).astype(BF16)
    nc = 512
    for c in range(CONV_DIM // nc):
        a_val = jnp.dot(h, w_ref[:, c * nc:(c + 1) * nc], preferred_element_type=F32)
        a_gate = jnp.dot(h, w_ref[:, CONV_DIM + c * nc:CONV_DIM + (c + 1) * nc],
                         preferred_element_type=F32)
        v_ref[:, c * nc:(c + 1) * nc] = a_val * jax.nn.sigmoid(a_gate)
        u_ref[:, c * nc:(c + 1) * nc] = jnp.dot(
            h, w_ref[:, 2 * CONV_DIM + c * nc:2 * CONV_DIM + (c + 1) * nc],
            preferred_element_type=F32)


def _mix_in(x, g, w_in, *, tm):
    rows = x.shape[0]
    return pl.pallas_call(
        _mix_in_kernel,
        out_shape=(jax.ShapeDtypeStruct((rows, CONV_DIM), F32),
                   jax.ShapeDtypeStruct((rows, POOL_DIM), F32)),
        grid=(rows // tm,),
        in_specs=[
            pl.BlockSpec((tm, D_MODEL), lambda i: (i, 0)),
            _resident((1, D_MODEL)),
            _resident(w_in.shape),
        ],
        out_specs=(pl.BlockSpec((tm, CONV_DIM), lambda i: (i, 0)),
                   pl.BlockSpec((tm, POOL_DIM), lambda i: (i, 0))),
        compiler_params=_params(("parallel",)),
        name="mix_in",
    )(x, g, w_in)


def _conv_post(acc, cb, lg, lb):
    y = acc + cb
    mu = jnp.mean(y, axis=-1, keepdims=True)
    yc = y - mu
    var = jnp.mean(yc * yc, axis=-1, keepdims=True)
    z = yc * lax.rsqrt(var + EPS) * lg + lb
    return z * jax.nn.sigmoid(z)


def _mix_tail(a_ref, d_ref, x_ref, pw_ref, ps_ref, wo_ref, o_ref):
    for g in range(len(POOL_WINDOWS)):
        sl = slice(g * POOL_GROUP_DIM, (g + 1) * POOL_GROUP_DIM)
        y = jnp.dot(d_ref[:, sl], pw_ref[g], preferred_element_type=F32) * ps_ref[:, sl]
        a_ref[:, CONV_DIM + g * POOL_GROUP_DIM:CONV_DIM + (g + 1) * POOL_GROUP_DIM] = y.astype(BF16)
    o_ref[...] = x_ref[...] + jnp.dot(a_ref[...], wo_ref[...], preferred_element_type=F32)


def _shifted_rows(win, n_out, offsets):
    n = win.shape[0]
    for b in range(SUBLANES):
        group = [o for o in offsets if o % SUBLANES == b]
        if not group:
            continue
        rolled = win if b == 0 else pltpu.roll(win, n - b, axis=0)
        for o in group:
            yield o, rolled[o - b:o - b + n_out]


def _mix_out_prompt_kernel(v_ref, vh_ref, u_ref, uh_ref, x_ref, cw_ref, cb_ref, lg_ref, lb_ref,
                           pw_ref, ps_ref, wo_ref, o_ref, vext_ref, uext_ref, a_ref, d_ref, *, tm):
    t = pl.program_id(1)
    first = t == 0

    vext_ref[0:CONV_HALO, :] = jnp.where(first, 0.0, vh_ref[...])
    vext_ref[CONV_HALO:, :] = v_ref[...]
    uext_ref[0:POOL_HALO, :] = jnp.where(first, 0.0, uh_ref[...])
    uext_ref[POOL_HALO:, :] = u_ref[...]

    cb, lg, lb = cb_ref[...], lg_ref[...], lb_ref[...]
    pos0 = t * tm

    def chunk(r, carry):
        r0 = pl.multiple_of(r * ROW_CHUNK, ROW_CHUNK)
        accs = []
        for c in range(CONV_DIM // LANE_GROUP):
            sl = slice(c * LANE_GROUP, (c + 1) * LANE_GROUP)
            win = vext_ref[pl.ds(r0, ROW_CHUNK + CONV_HALO), sl]
            offsets = [CONV_HALO - CONV_CTX + k for k in range(CONV_WIDTH)]
            acc = jnp.zeros((ROW_CHUNK, LANE_GROUP), F32)
            for o, rows in _shifted_rows(win, ROW_CHUNK, offsets):
                k = o - (CONV_HALO - CONV_CTX)
                acc = acc + cw_ref[k:k + 1, sl] * rows
            accs.append(acc)
        conv = _conv_post(jnp.concatenate(accs, axis=1), cb, lg, lb)
        a_ref[pl.ds(r0, ROW_CHUNK), 0:CONV_DIM] = conv.astype(BF16)

        pos = pos0 + r0 + lax.broadcasted_iota(jnp.int32, (ROW_CHUNK, 1), 0)
        for g, w in enumerate(POOL_WINDOWS):
            sl = slice(g * POOL_GROUP_DIM, (g + 1) * POOL_GROUP_DIM)
            win = uext_ref[pl.ds(r0, ROW_CHUNK + POOL_HALO), sl]
            shifted = dict(_shifted_rows(win, ROW_CHUNK, [POOL_HALO - i for i in range(w)]))
            cur = shifted[POOL_HALO]
            s = cur
            for i in range(1, w):
                s = s + shifted[POOL_HALO - i]
            cnt = jnp.minimum(pos + 1, w).astype(F32)
            d_ref[pl.ds(r0, ROW_CHUNK), sl] = (s / cnt - cur).astype(BF16)
        return carry

    lax.fori_loop(0, tm // ROW_CHUNK, chunk, 0)
    _mix_tail(a_ref, d_ref, x_ref, pw_ref, ps_ref, wo_ref, o_ref)


def _mix_out_prompt(v, u, x, cw, cb, lg, lb, pw, ps, wo, *, batch, seq, tm):
    nt = seq // tm
    vh_per_tile, uh_per_tile = tm // CONV_HALO, tm // POOL_HALO

    def row(b, t):
        return (b * nt + t, 0)

    def vhalo(b, t):
        return (jnp.maximum((b * nt + t) * vh_per_tile - 1, 0), 0)

    def uhalo(b, t):
        return (jnp.maximum((b * nt + t) * uh_per_tile - 1, 0), 0)

    return pl.pallas_call(
        functools.partial(_mix_out_prompt_kernel, tm=tm),
        out_shape=jax.ShapeDtypeStruct((batch * seq, D_MODEL), F32),
        grid=(batch, nt),
        in_specs=[
            pl.BlockSpec((tm, CONV_DIM), row),
            pl.BlockSpec((CONV_HALO, CONV_DIM), vhalo),
            pl.BlockSpec((tm, POOL_DIM), row),
            pl.BlockSpec((POOL_HALO, POOL_DIM), uhalo),
            pl.BlockSpec((tm, D_MODEL), row),
            _resident(cw.shape), _resident(cb.shape), _resident(lg.shape), _resident(lb.shape),
            _resident(pw.shape), _resident(ps.shape), _resident(wo.shape),
        ],
        out_specs=pl.BlockSpec((tm, D_MODEL), row),
        scratch_shapes=[
            pltpu.VMEM((CONV_HALO + tm, CONV_DIM), F32),
            pltpu.VMEM((POOL_HALO + tm, POOL_DIM), F32),
            pltpu.VMEM((tm, D_MODEL), BF16),
            pltpu.VMEM((tm, POOL_DIM), BF16),
        ],
        compiler_params=_params(("parallel", "arbitrary")),
        name="mix_out_prompt",
    )(v, v, u, u, x, cw, cb, lg, lb, pw, ps, wo)


def _mix_out_sample_kernel(v_ref, u_ref, sc_ref, sp_ref, x_ref, cw_ref, cb_ref, lg_ref, lb_ref,
                           pw_ref, ps_ref, wo_ref, o_ref, acc_ref, s_ref, a_ref, d_ref, *, nb):
    cw_ctx = cw_ref[0:CONV_CTX, :]
    rowi = lax.broadcasted_iota(jnp.int32, (POOL_CTX, POOL_DIM), 0)
    lane = lax.broadcasted_iota(jnp.int32, (POOL_CTX, POOL_DIM), 1)
    win = jnp.zeros((POOL_CTX, POOL_DIM), jnp.int32)
    for g, w in enumerate(POOL_WINDOWS):
        win = jnp.where(lane // POOL_GROUP_DIM == g, w, win)
    in_window = (rowi >= POOL_CTX + 1 - win).astype(F32)

    def per_seq(b, carry):
        acc_ref[pl.ds(b, 1), :] = jnp.sum(sc_ref[b] * cw_ctx, axis=0, keepdims=True)
        s_ref[pl.ds(b, 1), :] = jnp.sum(sp_ref[b] * in_window, axis=0, keepdims=True)
        return carry

    lax.fori_loop(0, nb, per_seq, 0)

    acc = acc_ref[...] + cw_ref[CONV_CTX:CONV_WIDTH, :] * v_ref[...]
    a_ref[:, 0:CONV_DIM] = _conv_post(acc, cb_ref[...], lg_ref[...], lb_ref[...]).astype(BF16)

    u = u_ref[...]
    s = s_ref[...] + u
    lane1 = lax.broadcasted_iota(jnp.int32, (1, POOL_DIM), 1)
    cnt = jnp.zeros((1, POOL_DIM), F32)
    for g, w in enumerate(POOL_WINDOWS):
        cnt = jnp.where(lane1 // POOL_GROUP_DIM == g, float(min(PAST_LEN + 1, w)), cnt)
    d_ref[...] = (s / cnt - u).astype(BF16)
    _mix_tail(a_ref, d_ref, x_ref, pw_ref, ps_ref, wo_ref, o_ref)


def _mix_out_sample(v, u, sc, sp, x, cw, cb, lg, lb, pw, ps, wo):
    nb = v.shape[0]
    args = (v, u, sc, sp, x, cw, cb, lg, lb, pw, ps, wo)
    return pl.pallas_call(
        functools.partial(_mix_out_sample_kernel, nb=nb),
        out_shape=jax.ShapeDtypeStruct((nb, D_MODEL), F32),
        grid=(1,),
        in_specs=[_resident(a.shape) for a in args],
        out_specs=pl.BlockSpec((nb, D_MODEL), lambda i: (0, 0)),
        scratch_shapes=[
            pltpu.VMEM((nb, CONV_DIM), F32),
            pltpu.VMEM((nb, POOL_DIM), F32),
            pltpu.VMEM((nb, D_MODEL), BF16),
            pltpu.VMEM((nb, POOL_DIM), BF16),
        ],
        compiler_params=_params(("arbitrary",)),
        name="mix_out_sample",
    )(*args)


def _ple_kernel(x_ref, p_ref, gp_ref, wg_ref, wp_ref, gf_ref, o_ref):
    x = x_ref[...]
    r = _rms(x, gp_ref[...]).astype(BF16)
    p = p_ref[...].astype(BF16)
    nc = 512
    for c in range(D_MODEL // nc):
        sl = slice(c * nc, (c + 1) * nc)
        gate = jax.nn.sigmoid(jnp.dot(r, wg_ref[:, sl], preferred_element_type=F32))
        proj = jnp.dot(p, wp_ref[:, sl], preferred_element_type=F32)
        o_ref[:, sl] = x_ref[:, sl] + gate * proj
    o_ref[...] = _rms(o_ref[...], gf_ref[...])


def _ple(x, p, gp, wg, wp, gf, *, tm):
    rows = x.shape[0]
    return pl.pallas_call(
        _ple_kernel,
        out_shape=jax.ShapeDtypeStruct((rows, D_MODEL), F32),
        grid=(rows // tm,),
        in_specs=[
            pl.BlockSpec((tm, D_MODEL), lambda i: (i, 0)),
            pl.BlockSpec((tm, PLE_DIM), lambda i: (i, 0)),
            _resident((1, D_MODEL)), _resident(wg.shape), _resident(wp.shape), _resident((1, D_MODEL)),
        ],
        out_specs=pl.BlockSpec((tm, D_MODEL), lambda i: (i, 0)),
        compiler_params=_params(("parallel",)),
        name="ple",
    )(x, p, gp, wg, wp, gf)


def kernel(x_prompt, x_sample, state_conv, state_pool, p_prompt, p_sample, norm_ffn1, w_ffn1_in, w_ffn1_out, norm_mix, w_in, conv_w, conv_b, conv_ln_g, conv_ln_b, pool_w, pool_scale, w_out, norm_ffn2, w_ffn2_in, w_ffn2_out, norm_ple, w_ple_gate, w_ple_proj, norm_final):
    assert norm_ffn1.shape[0] == 1, "the final norm is fused into the layer's last stage: one layer only"
    batch, seq, _ = x_prompt.shape
    nb = x_sample.shape[0]
    xp = x_prompt.reshape(batch * seq, D_MODEL)
    xs = x_sample.reshape(nb, D_MODEL)
    row = lambda a: a.reshape(1, -1)

    wi, wo = w_in[0].astype(BF16), w_out[0].astype(BF16)
    pw, wg, wp = pool_w[0].astype(BF16), w_ple_gate[0].astype(BF16), w_ple_proj[0].astype(BF16)
    g1, gm, g2, gp, gf = row(norm_ffn1[0]), row(norm_mix[0]), row(norm_ffn2[0]), row(norm_ple[0]), row(norm_final)
    mix = (conv_w[0], row(conv_b[0]), row(conv_ln_g[0]), row(conv_ln_b[0]), pw, row(pool_scale[0]), wo)

    x1, x1s = _ffn(xp, xs, g1, w_ffn1_in[0], w_ffn1_out[0], tm=FFN_TM, tf=FFN_TF)
    v, u = _mix_in(x1, gm, wi, tm=PROMPT_TM)
    vs, us = _mix_in(x1s, gm, wi, tm=nb)
    x2 = _mix_out_prompt(v, u, x1, *mix, batch=batch, seq=seq, tm=PROMPT_TM)
    x2s = _mix_out_sample(vs, us, state_conv[0], state_pool[0], x1s, *mix)
    x3, x3s = _ffn(x2, x2s, g2, w_ffn2_in[0], w_ffn2_out[0], tm=FFN_TM, tf=FFN_TF)
    yp = _ple(x3, p_prompt[0].reshape(batch * seq, PLE_DIM), gp, wg, wp, gf, tm=PROMPT_TM)
    ys = _ple(x3s, p_sample[0].reshape(nb, PLE_DIM), gp, wg, wp, gf, tm=nb)
    new_conv_p = v.reshape(batch, seq, CONV_DIM)[:, seq - CONV_CTX:]
    new_pool_p = u.reshape(batch, seq, POOL_DIM)[:, seq - POOL_CTX:]
    new_conv_s = jnp.concatenate([state_conv[0][:, 1:], vs[:, None, :]], axis=1)
    new_pool_s = jnp.concatenate([state_pool[0][:, 1:], us[:, None, :]], axis=1)

    return (yp.reshape(batch, seq, D_MODEL), ys.reshape(nb, 1, D_MODEL),
            new_conv_p[None], new_conv_s[None], new_pool_p[None], new_pool_s[None])
```

```python
import functools

import jax
import jax.numpy as jnp
from jax import lax
from jax.experimental import pallas as pl
from jax.experimental.pallas import tpu as pltpu

D_MODEL = 2048
D_FF = 5632
CONV_DIM = 1024
POOL_DIM = 1024
POOL_WINDOWS = (2, 4, 8, 16)
POOL_GROUP_DIM = POOL_DIM // len(POOL_WINDOWS)
POOL_CTX = max(POOL_WINDOWS) - 1
CONV_WIDTH = 31
CONV_CTX = CONV_WIDTH - 1
PLE_DIM = 256
PAST_LEN = 16384
EPS = 1e-6

F32 = jnp.float32
BF16 = jnp.bfloat16

VMEM_LIMIT_BYTES = 62 * 1024 * 1024
PROMPT_TM = 512
FFN_TM = 1024
FFN_TF = 512
CONV_HALO = 32
POOL_HALO = 16
ROW_CHUNK = 64
LANE_GROUP = 256
SUBLANES = 8


def _rms(x, g):
    ms = jnp.mean(x * x, axis=-1, keepdims=True)
    return x * lax.rsqrt(ms + EPS) * g


def _params(sem):
    return pltpu.CompilerParams(dimension_semantics=sem, vmem_limit_bytes=VMEM_LIMIT_BYTES)


def _resident(shape):
    nd = len(shape)
    return pl.BlockSpec(shape, lambda *_: (0,) * nd, pipeline_mode=pl.Buffered(1))


def _ffn_rows(j, x_ref, g_ref, wg_ref, wu_ref, wo_ref, o_ref, h_ref):
    @pl.when(j == 0)
    def _():
        x = x_ref[...]
        h_ref[...] = _rms(x, g_ref[...]).astype(BF16)
        o_ref[...] = x

    h = h_ref[...]
    gate = jnp.dot(h, wg_ref[...].astype(BF16), preferred_element_type=F32)
    up = jnp.dot(h, wu_ref[...].astype(BF16), preferred_element_type=F32)
    act = (gate * jax.nn.sigmoid(gate) * up * 0.5).astype(BF16)
    o_ref[...] += jnp.dot(act, wo_ref[...].astype(BF16), preferred_element_type=F32)


def _ffn_kernel(xp_ref, xs_ref, g_ref, wg_ref, wu_ref, wo_ref, op_ref, os_ref, h_ref, *, tm):
    i, j = pl.program_id(0), pl.program_id(1)
    ns = xs_ref.shape[0]
    _ffn_rows(j, xp_ref, g_ref, wg_ref, wu_ref, wo_ref, op_ref, h_ref.at[0:tm])

    @pl.when(i == pl.num_programs(0) - 1)
    def _():
        _ffn_rows(j, xs_ref, g_ref, wg_ref, wu_ref, wo_ref, os_ref, h_ref.at[tm:tm + ns])


def _ffn(xp, xs, g, w_in, w_out, *, tm, tf):
    ns = xs.shape[0]
    nf = D_FF // tf
    return pl.pallas_call(
        functools.partial(_ffn_kernel, tm=tm),
        out_shape=(jax.ShapeDtypeStruct(xp.shape, F32), jax.ShapeDtypeStruct(xs.shape, F32)),
        grid=(xp.shape[0] // tm, nf),
        in_specs=[
            pl.BlockSpec((tm, D_MODEL), lambda i, j: (i, 0), pipeline_mode=pl.Buffered(1)),
            pl.BlockSpec((ns, D_MODEL), lambda i, j: (0, 0), pipeline_mode=pl.Buffered(1)),
            pl.BlockSpec((1, D_MODEL), lambda i, j: (0, 0), pipeline_mode=pl.Buffered(1)),
            pl.BlockSpec((D_MODEL, tf), lambda i, j: (0, j)),
            pl.BlockSpec((D_MODEL, tf), lambda i, j: (0, j + nf)),
            pl.BlockSpec((tf, D_MODEL), lambda i, j: (j, 0)),
        ],
        out_specs=(pl.BlockSpec((tm, D_MODEL), lambda i, j: (i, 0)),
                   pl.BlockSpec((ns, D_MODEL), lambda i, j: (0, 0))),
        scratch_shapes=[pltpu.VMEM((tm + ns, D_MODEL), BF16)],
        compiler_params=_params(("arbitrary", "arbitrary")),
        name="ffn",
    )(xp, xs, g, w_in, w_in, w_out)


def _mix_in_kernel(x_ref, g_ref, w_ref, v_ref, u_ref):
    h = _rms(x_ref[...], g_ref[...]).astype(BF16)
    nc = 512
    for c in range(CONV_DIM // nc):
        a_val = jnp.dot(h, w_ref[:, c * nc:(c + 1) * nc], preferred_element_type=F32)
        a_gate = jnp.dot(h, w_ref[:, CONV_DIM + c * nc:CONV_DIM + (c + 1) * nc],
                         preferred_element_type=F32)
        v_ref[:, c * nc:(c + 1) * nc] = a_val * jax.nn.sigmoid(a_gate)
        u_ref[:, c * nc:(c + 1) * nc] = jnp.dot(
            h, w_ref[:, 2 * CONV_DIM + c * nc:2 * CONV_DIM + (c + 1) * nc],
            preferred_element_type=F32)


def _mix_in(x, g, w_in, *, tm):
    rows = x.shape[0]
    return pl.pallas_call(
        _mix_in_kernel,
        out_shape=(jax.ShapeDtypeStruct((rows, CONV_DIM), F32),
                   jax.ShapeDtypeStruct((rows, POOL_DIM), F32)),
        grid=(rows // tm,),
        in_specs=[
            pl.BlockSpec((tm, D_MODEL), lambda i: (i, 0)),
            _resident((1, D_MODEL)),
            _resident(w_in.shape),
        ],
        out_specs=(pl.BlockSpec((tm, CONV_DIM), lambda i: (i, 0)),
                   pl.BlockSpec((tm, POOL_DIM), lambda i: (i, 0))),
        compiler_params=_params(("parallel",)),
        name="mix_in",
    )(x, g, w_in)


def _conv_post(acc, cb, lg, lb):
    y = acc + cb
    mu = jnp.mean(y, axis=-1, keepdims=True)
    yc = y - mu
    var = jnp.mean(yc * yc, axis=-1, keepdims=True)
    z = yc * lax.rsqrt(var + EPS) * lg + lb
    return z * jax.nn.sigmoid(z)


def _mix_tail(a_ref, d_ref, x_ref, pw_ref, ps_ref, wo_ref, o_ref):
    for g in range(len(POOL_WINDOWS)):
        sl = slice(g * POOL_GROUP_DIM, (g + 1) * POOL_GROUP_DIM)
        y = jnp.dot(d_ref[:, sl], pw_ref[g], preferred_element_type=F32) * ps_ref[:, sl]
        a_ref[:, CONV_DIM + g * POOL_GROUP_DIM:CONV_DIM + (g + 1) * POOL_GROUP_DIM] = y.astype(BF16)
    o_ref[...] = x_ref[...] + jnp.dot(a_ref[...], wo_ref[...], preferred_element_type=F32)


def _rows_above(tiles, b):
    rot = [pltpu.roll(t, SUBLANES - b, axis=0) for t in tiles]
    own = lax.broadcasted_iota(jnp.int32, tiles[0].shape, 0) < SUBLANES - b
    return [jnp.where(own, rot[j], rot[j + 1]) for j in range(len(tiles) - 1)]


def _rows_below(tiles, d):
    rot = [pltpu.roll(t, d, axis=0) for t in tiles]
    own = lax.broadcasted_iota(jnp.int32, tiles[0].shape, 0) >= d
    return [jnp.where(own, rot[j], rot[max(j - 1, 0)]) for j in range(len(tiles))]


def _conv_chunk(v_ref, wb_ref, r0, sl):
    n_out = ROW_CHUNK // SUBLANES
    n_src = n_out + CONV_HALO // SUBLANES
    src = [v_ref[r0 + SUBLANES * j:r0 + SUBLANES * (j + 1), sl] for j in range(n_src)]
    acc = [None] * n_out
    for b in range(SUBLANES):
        taps = [k for k in range(CONV_WIDTH) if (CONV_HALO - CONV_CTX + k) % SUBLANES == b]
        if not taps:
            continue
        shifted = src if b == 0 else _rows_above(src, b)
        for k in taps:
            a = (CONV_HALO - CONV_CTX + k) // SUBLANES
            wk = wb_ref[k, :, sl]
            for i in range(n_out):
                term = wk * shifted[a + i]
                acc[i] = term if acc[i] is None else acc[i] + term
    return jnp.concatenate(acc, axis=0)


def _pool_chunk(u_ref, r0, pos0, d_ref):
    n_out = ROW_CHUNK // SUBLANES
    n_ctx = POOL_HALO // SUBLANES
    for g, w in enumerate(POOL_WINDOWS):
        sl = slice(g * POOL_GROUP_DIM, (g + 1) * POOL_GROUP_DIM)
        cur = [u_ref[r0 + SUBLANES * j:r0 + SUBLANES * (j + 1), sl] for j in range(n_out + n_ctx)]
        s, span = cur, 1
        while span < w:
            if span < SUBLANES:
                below = _rows_below(s, span)
            else:
                below = [s[0]] + s[:-1]
            s = [x + y for x, y in zip(s, below)]
            span *= 2
        d = []
        for i in range(n_out):
            pos = pos0 + r0 + SUBLANES * i + lax.broadcasted_iota(jnp.int32, cur[0].shape, 0)
            cnt = jnp.minimum(pos + 1, w).astype(F32)
            d.append(s[n_ctx + i] / cnt - cur[n_ctx + i])
        d_ref[r0:r0 + ROW_CHUNK, sl] = jnp.concatenate(d, axis=0).astype(BF16)


def _mix_prompt_kernel(x_ref, g_ref, w_ref, cw_ref, cb_ref, lg_ref, lb_ref, pw_ref, ps_ref,
                       a_ref, ctail_ref, ptail_ref, vext_ref, uext_ref, d_ref, wb_ref, *, tm, tiles_per_seq):
    s = pl.program_id(0)
    n_tiles = pl.num_programs(0) - 1
    slot = lax.rem(s, 2)
    v_new, u_new = vext_ref.at[slot], uext_ref.at[slot]
    v_old, u_old = vext_ref.at[1 - slot], uext_ref.at[1 - slot]

    @pl.when(s == 0)
    def _():
        vext_ref[1] = jnp.zeros(vext_ref.shape[1:], F32)
        uext_ref[1] = jnp.zeros(uext_ref.shape[1:], F32)
        for k in range(CONV_WIDTH):
            wb_ref[k] = jnp.broadcast_to(cw_ref[k:k + 1, :], wb_ref.shape[1:])

    h = _rms(x_ref[...], g_ref[...]).astype(BF16)
    nc = 512
    for c in range(CONV_DIM // nc):
        sl = slice(c * nc, (c + 1) * nc)
        a_val = jnp.dot(h, w_ref[:, sl], preferred_element_type=F32)
        a_gate = jnp.dot(h, w_ref[:, CONV_DIM + c * nc:CONV_DIM + (c + 1) * nc], preferred_element_type=F32)
        v_new[CONV_HALO:, sl] = a_val * jax.nn.sigmoid(a_gate)
        u_new[POOL_HALO:, sl] = jnp.dot(h, w_ref[:, 2 * CONV_DIM + c * nc:2 * CONV_DIM + (c + 1) * nc],
                                       preferred_element_type=F32)
    starts_seq = lax.rem(jnp.minimum(s, n_tiles - 1), tiles_per_seq) == 0
    v_new[0:CONV_HALO, :] = jnp.where(starts_seq, 0.0, v_old[tm:tm + CONV_HALO, :])
    u_new[0:POOL_HALO, :] = jnp.where(starts_seq, 0.0, u_old[tm:tm + POOL_HALO, :])

    pos0 = lax.rem(jnp.maximum(s - 1, 0), tiles_per_seq) * tm
    cb, lg, lb = cb_ref[...], lg_ref[...], lb_ref[...]
    for r in range(tm // ROW_CHUNK):
        r0 = r * ROW_CHUNK
        conv = jnp.concatenate(
            [_conv_chunk(v_old, wb_ref, r0, slice(c * LANE_GROUP, (c + 1) * LANE_GROUP))
             for c in range(CONV_DIM // LANE_GROUP)], axis=1)
        a_ref[r0:r0 + ROW_CHUNK, 0:CONV_DIM] = _conv_post(conv, cb, lg, lb).astype(BF16)
        _pool_chunk(u_old, r0, pos0, d_ref)
    for g in range(len(POOL_WINDOWS)):
        sl = slice(g * POOL_GROUP_DIM, (g + 1) * POOL_GROUP_DIM)
        y = jnp.dot(d_ref[:, sl], pw_ref[g], preferred_element_type=F32) * ps_ref[:, sl]
        a_ref[:, CONV_DIM + g * POOL_GROUP_DIM:CONV_DIM + (g + 1) * POOL_GROUP_DIM] = y.astype(BF16)
    ctail_ref[0] = v_old[tm:tm + CONV_HALO, :]
    ptail_ref[0] = u_old[tm:tm + POOL_HALO, :]


def _mix_prompt(x, g, w_in, cw, cb, lg, lb, pw, ps, *, batch, seq, tm):
    tiles_per_seq = seq // tm
    n_tiles = batch * tiles_per_seq
    cur = lambda s: (jnp.minimum(s, n_tiles - 1), 0)
    prev = lambda s: (jnp.maximum(s - 1, 0), 0)
    prev_seq = lambda s: (jnp.maximum(s - 1, 0) // tiles_per_seq, 0, 0)
    return pl.pallas_call(
        functools.partial(_mix_prompt_kernel, tm=tm, tiles_per_seq=tiles_per_seq),
        out_shape=(jax.ShapeDtypeStruct((batch * seq, D_MODEL), BF16),
                   jax.ShapeDtypeStruct((batch, CONV_HALO, CONV_DIM), F32),
                   jax.ShapeDtypeStruct((batch, POOL_HALO, POOL_DIM), F32)),
        grid=(n_tiles + 1,),
        in_specs=[
            pl.BlockSpec((tm, D_MODEL), cur),
            _resident(g.shape), _resident(w_in.shape), _resident(cw.shape), _resident(cb.shape),
            _resident(lg.shape), _resident(lb.shape), _resident(pw.shape), _resident(ps.shape),
        ],
        out_specs=(pl.BlockSpec((tm, D_MODEL), prev),
                   pl.BlockSpec((1, CONV_HALO, CONV_DIM), prev_seq),
                   pl.BlockSpec((1, POOL_HALO, POOL_DIM), prev_seq)),
        scratch_shapes=[
            pltpu.VMEM((2, CONV_HALO + tm, CONV_DIM), F32),
            pltpu.VMEM((2, POOL_HALO + tm, POOL_DIM), F32),
            pltpu.VMEM((tm, POOL_DIM), BF16),
            pltpu.VMEM((CONV_WIDTH, SUBLANES, CONV_DIM), F32),
        ],
        compiler_params=_params(("arbitrary",)),
        name="mix_prompt",
    )(x, g, w_in, cw, cb, lg, lb, pw, ps)


def _out_proj_kernel(x_ref, a_ref, wo_ref, o_ref):
    o_ref[...] = x_ref[...] + jnp.dot(a_ref[...], wo_ref[...], preferred_element_type=F32)


def _out_proj(x, a, wo, *, tm):
    rows = x.shape[0]
    return pl.pallas_call(
        _out_proj_kernel,
        out_shape=jax.ShapeDtypeStruct((rows, D_MODEL), F32),
        grid=(rows // tm,),
        in_specs=[pl.BlockSpec((tm, D_MODEL), lambda i: (i, 0)),
                  pl.BlockSpec((tm, D_MODEL), lambda i: (i, 0)),
                  _resident(wo.shape)],
        out_specs=pl.BlockSpec((tm, D_MODEL), lambda i: (i, 0)),
        compiler_params=_params(("parallel",)),
        name="out_proj",
    )(x, a, wo)


def _mix_out_sample_kernel(v_ref, u_ref, sc_ref, sp_ref, x_ref, cw_ref, cb_ref, lg_ref, lb_ref,
                           pw_ref, ps_ref, wo_ref, o_ref, acc_ref, s_ref, a_ref, d_ref, *, nb):
    cw_ctx = cw_ref[0:CONV_CTX, :]
    rowi = lax.broadcasted_iota(jnp.int32, (POOL_CTX, POOL_DIM), 0)
    lane = lax.broadcasted_iota(jnp.int32, (POOL_CTX, POOL_DIM), 1)
    win = jnp.zeros((POOL_CTX, POOL_DIM), jnp.int32)
    for g, w in enumerate(POOL_WINDOWS):
        win = jnp.where(lane // POOL_GROUP_DIM == g, w, win)
    in_window = (rowi >= POOL_CTX + 1 - win).astype(F32)

    def per_seq(b, carry):
        acc_ref[pl.ds(b, 1), :] = jnp.sum(sc_ref[b] * cw_ctx, axis=0, keepdims=True)
        s_ref[pl.ds(b, 1), :] = jnp.sum(sp_ref[b] * in_window, axis=0, keepdims=True)
        return carry

    lax.fori_loop(0, nb, per_seq, 0)

    acc = acc_ref[...] + cw_ref[CONV_CTX:CONV_WIDTH, :] * v_ref[...]
    a_ref[:, 0:CONV_DIM] = _conv_post(acc, cb_ref[...], lg_ref[...], lb_ref[...]).astype(BF16)

    u = u_ref[...]
    s = s_ref[...] + u
    lane1 = lax.broadcasted_iota(jnp.int32, (1, POOL_DIM), 1)
    cnt = jnp.zeros((1, POOL_DIM), F32)
    for g, w in enumerate(POOL_WINDOWS):
        cnt = jnp.where(lane1 // POOL_GROUP_DIM == g, float(min(PAST_LEN + 1, w)), cnt)
    d_ref[...] = (s / cnt - u).astype(BF16)
    _mix_tail(a_ref, d_ref, x_ref, pw_ref, ps_ref, wo_ref, o_ref)


def _mix_out_sample(v, u, sc, sp, x, cw, cb, lg, lb, pw, ps, wo):
    nb = v.shape[0]
    args = (v, u, sc, sp, x, cw, cb, lg, lb, pw, ps, wo)
    return pl.pallas_call(
        functools.partial(_mix_out_sample_kernel, nb=nb),
        out_shape=jax.ShapeDtypeStruct((nb, D_MODEL), F32),
        grid=(1,),
        in_specs=[_resident(a.shape) for a in args],
        out_specs=pl.BlockSpec((nb, D_MODEL), lambda i: (0, 0)),
        scratch_shapes=[
            pltpu.VMEM((nb, CONV_DIM), F32),
            pltpu.VMEM((nb, POOL_DIM), F32),
            pltpu.VMEM((nb, D_MODEL), BF16),
            pltpu.VMEM((nb, POOL_DIM), BF16),
        ],
        compiler_params=_params(("arbitrary",)),
        name="mix_out_sample",
    )(*args)


def _ple_kernel(x_ref, p_ref, gp_ref, wg_ref, wp_ref, gf_ref, o_ref):
    x = x_ref[...]
    r = _rms(x, gp_ref[...]).astype(BF16)
    p = p_ref[...].astype(BF16)
    nc = 512
    for c in range(D_MODEL // nc):
        sl = slice(c * nc, (c + 1) * nc)
        gate = jax.nn.sigmoid(jnp.dot(r, wg_ref[:, sl], preferred_element_type=F32))
        proj = jnp.dot(p, wp_ref[:, sl], preferred_element_type=F32)
        o_ref[:, sl] = x_ref[:, sl] + gate * proj
    o_ref[...] = _rms(o_ref[...], gf_ref[...])


def _ple(x, p, gp, wg, wp, gf, *, tm):
    rows = x.shape[0]
    return pl.pallas_call(
        _ple_kernel,
        out_shape=jax.ShapeDtypeStruct((rows, D_MODEL), F32),
        grid=(rows // tm,),
        in_specs=[
            pl.BlockSpec((tm, D_MODEL), lambda i: (i, 0)),
            pl.BlockSpec((tm, PLE_DIM), lambda i: (i, 0)),
            _resident((1, D_MODEL)), _resident(wg.shape), _resident(wp.shape), _resident((1, D_MODEL)),
        ],
        out_specs=pl.BlockSpec((tm, D_MODEL), lambda i: (i, 0)),
        compiler_params=_params(("parallel",)),
        name="ple",
    )(x, p, gp, wg, wp, gf)


def kernel(x_prompt, x_sample, state_conv, state_pool, p_prompt, p_sample, norm_ffn1, w_ffn1_in, w_ffn1_out, norm_mix, w_in, conv_w, conv_b, conv_ln_g, conv_ln_b, pool_w, pool_scale, w_out, norm_ffn2, w_ffn2_in, w_ffn2_out, norm_ple, w_ple_gate, w_ple_proj, norm_final):
    assert norm_ffn1.shape[0] == 1, "the final norm is fused into the layer's last stage: one layer only"
    batch, seq, _ = x_prompt.shape
    nb = x_sample.shape[0]
    xp = x_prompt.reshape(batch * seq, D_MODEL)
    xs = x_sample.reshape(nb, D_MODEL)
    row = lambda a: a.reshape(1, -1)

    wi, wo = w_in[0].astype(BF16), w_out[0].astype(BF16)
    pw, wg, wp = pool_w[0].astype(BF16), w_ple_gate[0].astype(BF16), w_ple_proj[0].astype(BF16)
    g1, gm, g2, gp, gf = row(norm_ffn1[0]), row(norm_mix[0]), row(norm_ffn2[0]), row(norm_ple[0]), row(norm_final)
    mix = (conv_w[0], row(conv_b[0]), row(conv_ln_g[0]), row(conv_ln_b[0]), pw, row(pool_scale[0]), wo)

    x1, x1s = _ffn(xp, xs, g1, w_ffn1_in[0], w_ffn1_out[0], tm=FFN_TM, tf=FFN_TF)
    a, v_tail, u_tail = _mix_prompt(x1, gm, wi, *mix[:-1], batch=batch, seq=seq, tm=PROMPT_TM)
    x2 = _out_proj(x1, a, wo, tm=PROMPT_TM)
    vs, us = _mix_in(x1s, gm, wi, tm=nb)
    x2s = _mix_out_sample(vs, us, state_conv[0], state_pool[0], x1s, *mix)
    x3, x3s = _ffn(x2, x2s, g2, w_ffn2_in[0], w_ffn2_out[0], tm=FFN_TM, tf=FFN_TF)
    yp = _ple(x3, p_prompt[0].reshape(batch * seq, PLE_DIM), gp, wg, wp, gf, tm=PROMPT_TM)
    ys = _ple(x3s, p_sample[0].reshape(nb, PLE_DIM), gp, wg, wp, gf, tm=nb)
    new_conv_p = v_tail[:, CONV_HALO - CONV_CTX:]
    new_pool_p = u_tail[:, POOL_HALO - POOL_CTX:]
    new_conv_s = jnp.concatenate([state_conv[0][:, 1:], vs[:, None, :]], axis=1)
    new_pool_s = jnp.concatenate([state_pool[0][:, 1:], us[:, None, :]], axis=1)

    return (yp.reshape(batch, seq, D_MODEL), ys.reshape(nb, 1, D_MODEL),
            new_conv_p[None], new_conv_s[None], new_pool_p[None], new_pool_s[None])
```

```python
import functools

import jax
import jax.numpy as jnp
from jax import lax
from jax.experimental import pallas as pl
from jax.experimental.pallas import tpu as pltpu

D_MODEL = 2048
D_FF = 5632
CONV_DIM = 1024
POOL_DIM = 1024
POOL_WINDOWS = (2, 4, 8, 16)
POOL_GROUP_DIM = POOL_DIM // len(POOL_WINDOWS)
POOL_CTX = max(POOL_WINDOWS) - 1
CONV_WIDTH = 31
CONV_CTX = CONV_WIDTH - 1
PLE_DIM = 256
PAST_LEN = 16384
EPS = 1e-6

F32 = jnp.float32
BF16 = jnp.bfloat16

VMEM_LIMIT_BYTES = 62 * 1024 * 1024
PROMPT_TM = 512
FFN_TM = 1024
FFN_TF = 512
CONV_HALO = 32
POOL_HALO = 16
ROW_CHUNK = 64
LANE_GROUP = 256
SUBLANES = 8


def _rms(x, g):
    ms = jnp.mean(x * x, axis=-1, keepdims=True)
    return x * lax.rsqrt(ms + EPS) * g


def _params(sem):
    return pltpu.CompilerParams(dimension_semantics=sem, vmem_limit_bytes=VMEM_LIMIT_BYTES)


def _resident(shape):
    nd = len(shape)
    return pl.BlockSpec(shape, lambda *_: (0,) * nd, pipeline_mode=pl.Buffered(1))


def _ffn_rows(j, x_ref, g_ref, w_refs, o_ref, h_ref, bf16_copy_refs=(None, None, None)):
    @pl.when(j == 0)
    def _():
        x = x_ref[...]
        h_ref[...] = _rms(x, g_ref[...]).astype(BF16)
        o_ref[...] = x

    def weight(k):
        w = w_refs[k][...].astype(BF16)
        if bf16_copy_refs[k] is not None:
            bf16_copy_refs[k][...] = w
        return w

    h = h_ref[...]
    gate = jnp.dot(h, weight(0), preferred_element_type=F32)
    up = jnp.dot(h, weight(1), preferred_element_type=F32)
    act = (gate * jax.nn.sigmoid(gate) * up * 0.5).astype(BF16)
    o_ref[...] += jnp.dot(act, weight(2), preferred_element_type=F32)


def _ffn_head_kernel(x_ref, g_ref, wg_ref, wu_ref, wo_ref, o_ref, wgb_ref, wub_ref, wob_ref, h_ref):
    _ffn_rows(pl.program_id(0), x_ref, g_ref, (wg_ref, wu_ref, wo_ref), o_ref, h_ref,
              bf16_copy_refs=(wgb_ref, wub_ref, wob_ref))


def _ffn_body_kernel(head_ref, xp_ref, xs_ref, g_ref, wg_ref, wu_ref, wo_ref, op_ref, os_ref, h_ref, *, tm):
    del head_ref
    i, j = pl.program_id(0), pl.program_id(1)
    ns = xs_ref.shape[0]
    w_refs = (wg_ref, wu_ref, wo_ref)
    _ffn_rows(j, xp_ref, g_ref, w_refs, op_ref, h_ref.at[0:tm])

    @pl.when(i == pl.num_programs(0) - 1)
    def _():
        _ffn_rows(j, xs_ref, g_ref, w_refs, os_ref, h_ref.at[tm:tm + ns])


def _ffn(xp, xs, g, w_in, w_out, *, tm, tf):
    ns = xs.shape[0]
    nf = D_FF // tf
    hf = tf // 2
    nh = D_FF // hf
    once = dict(pipeline_mode=pl.Buffered(1))
    head, wg, wu, wo = pl.pallas_call(
        _ffn_head_kernel,
        out_shape=(jax.ShapeDtypeStruct(xp.shape, F32),
                   jax.ShapeDtypeStruct((D_MODEL, D_FF), BF16),
                   jax.ShapeDtypeStruct((D_MODEL, D_FF), BF16),
                   jax.ShapeDtypeStruct((D_FF, D_MODEL), BF16)),
        grid=(nh,),
        in_specs=[
            pl.BlockSpec((tm, D_MODEL), lambda j: (0, 0), **once),
            pl.BlockSpec((1, D_MODEL), lambda j: (0, 0), **once),
            pl.BlockSpec((D_MODEL, hf), lambda j: (0, j)),
            pl.BlockSpec((D_MODEL, hf), lambda j: (0, j + nh)),
            pl.BlockSpec((hf, D_MODEL), lambda j: (j, 0)),
        ],
        out_specs=(pl.BlockSpec((tm, D_MODEL), lambda j: (0, 0)),
                   pl.BlockSpec((D_MODEL, hf), lambda j: (0, j)),
                   pl.BlockSpec((D_MODEL, hf), lambda j: (0, j)),
                   pl.BlockSpec((hf, D_MODEL), lambda j: (j, 0))),
        scratch_shapes=[pltpu.VMEM((tm, D_MODEL), BF16)],
        compiler_params=_params(("arbitrary",)),
        name="ffn_head",
    )(xp, g, w_in, w_in, w_out)
    return pl.pallas_call(
        functools.partial(_ffn_body_kernel, tm=tm),
        out_shape=(jax.ShapeDtypeStruct(xp.shape, F32), jax.ShapeDtypeStruct(xs.shape, F32)),
        grid=(xp.shape[0] // tm - 1, nf),
        in_specs=[
            pl.BlockSpec(memory_space=pl.ANY),
            pl.BlockSpec((tm, D_MODEL), lambda i, j: (i + 1, 0)),
            pl.BlockSpec((ns, D_MODEL), lambda i, j: (0, 0), **once),
            pl.BlockSpec((1, D_MODEL), lambda i, j: (0, 0), **once),
            pl.BlockSpec((D_MODEL, tf), lambda i, j: (0, j)),
            pl.BlockSpec((D_MODEL, tf), lambda i, j: (0, j)),
            pl.BlockSpec((tf, D_MODEL), lambda i, j: (j, 0)),
        ],
        out_specs=(pl.BlockSpec((tm, D_MODEL), lambda i, j: (i + 1, 0)),
                   pl.BlockSpec((ns, D_MODEL), lambda i, j: (0, 0))),
        scratch_shapes=[pltpu.VMEM((tm + ns, D_MODEL), BF16)],
        input_output_aliases={0: 0},
        compiler_params=_params(("arbitrary", "arbitrary")),
        name="ffn_body",
    )(head, xp, xs, g, wg, wu, wo)


def _mix_in_kernel(x_ref, g_ref, w_ref, v_ref, u_ref):
    h = _rms(x_ref[...], g_ref[...]).astype(BF16)
    nc = 512
    for c in range(CONV_DIM // nc):
        a_val = jnp.dot(h, w_ref[:, c * nc:(c + 1) * nc], preferred_element_type=F32)
        a_gate = jnp.dot(h, w_ref[:, CONV_DIM + c * nc:CONV_DIM + (c + 1) * nc],
                         preferred_element_type=F32)
        v_ref[:, c * nc:(c + 1) * nc] = a_val * jax.nn.sigmoid(a_gate)
        u_ref[:, c * nc:(c + 1) * nc] = jnp.dot(
            h, w_ref[:, 2 * CONV_DIM + c * nc:2 * CONV_DIM + (c + 1) * nc],
            preferred_element_type=F32)


def _mix_in(x, g, w_in, *, tm):
    rows = x.shape[0]
    return pl.pallas_call(
        _mix_in_kernel,
        out_shape=(jax.ShapeDtypeStruct((rows, CONV_DIM), F32),
                   jax.ShapeDtypeStruct((rows, POOL_DIM), F32)),
        grid=(rows // tm,),
        in_specs=[
            pl.BlockSpec((tm, D_MODEL), lambda i: (i, 0)),
            _resident((1, D_MODEL)),
            _resident(w_in.shape),
        ],
        out_specs=(pl.BlockSpec((tm, CONV_DIM), lambda i: (i, 0)),
                   pl.BlockSpec((tm, POOL_DIM), lambda i: (i, 0))),
        compiler_params=_params(("parallel",)),
        name="mix_in",
    )(x, g, w_in)


def _conv_post(acc, cb, lg, lb):
    y = acc + cb
    mu = jnp.mean(y, axis=-1, keepdims=True)
    yc = y - mu
    var = jnp.mean(yc * yc, axis=-1, keepdims=True)
    z = yc * lax.rsqrt(var + EPS) * lg + lb
    return z * jax.nn.sigmoid(z)


def _mix_tail(a_ref, d_ref, x_ref, pw_ref, ps_ref, wo_ref, o_ref):
    for g in range(len(POOL_WINDOWS)):
        sl = slice(g * POOL_GROUP_DIM, (g + 1) * POOL_GROUP_DIM)
        y = jnp.dot(d_ref[:, sl], pw_ref[g], preferred_element_type=F32) * ps_ref[:, sl]
        a_ref[:, CONV_DIM + g * POOL_GROUP_DIM:CONV_DIM + (g + 1) * POOL_GROUP_DIM] = y.astype(BF16)
    o_ref[...] = x_ref[...] + jnp.dot(a_ref[...], wo_ref[...], preferred_element_type=F32)


def _rows_above(tiles, b):
    rot = [pltpu.roll(t, SUBLANES - b, axis=0) for t in tiles]
    own = lax.broadcasted_iota(jnp.int32, tiles[0].shape, 0) < SUBLANES - b
    return [jnp.where(own, rot[j], rot[j + 1]) for j in range(len(tiles) - 1)]


def _rows_below(tiles, d):
    rot = [pltpu.roll(t, d, axis=0) for t in tiles]
    own = lax.broadcasted_iota(jnp.int32, tiles[0].shape, 0) >= d
    return [jnp.where(own, rot[j], rot[max(j - 1, 0)]) for j in range(len(tiles))]


def _conv_chunk(v_ref, wb_ref, r0, sl):
    n_out = ROW_CHUNK // SUBLANES
    n_src = n_out + CONV_HALO // SUBLANES
    src = [v_ref[r0 + SUBLANES * j:r0 + SUBLANES * (j + 1), sl] for j in range(n_src)]
    acc = [None] * n_out
    for b in range(SUBLANES):
        taps = [k for k in range(CONV_WIDTH) if (CONV_HALO - CONV_CTX + k) % SUBLANES == b]
        if not taps:
            continue
        shifted = src if b == 0 else _rows_above(src, b)
        for k in taps:
            a = (CONV_HALO - CONV_CTX + k) // SUBLANES
            wk = wb_ref[k, :, sl]
            for i in range(n_out):
                term = wk * shifted[a + i]
                acc[i] = term if acc[i] is None else acc[i] + term
    return jnp.concatenate(acc, axis=0)


def _pool_chunk(u_ref, r0, pos0, d_ref):
    n_out = ROW_CHUNK // SUBLANES
    n_ctx = POOL_HALO // SUBLANES
    for g, w in enumerate(POOL_WINDOWS):
        sl = slice(g * POOL_GROUP_DIM, (g + 1) * POOL_GROUP_DIM)
        cur = [u_ref[r0 + SUBLANES * j:r0 + SUBLANES * (j + 1), sl] for j in range(n_out + n_ctx)]
        s, span = cur, 1
        while span < w:
            if span < SUBLANES:
                below = _rows_below(s, span)
            else:
                below = [s[0]] + s[:-1]
            s = [x + y for x, y in zip(s, below)]
            span *= 2
        d = []
        for i in range(n_out):
            pos = pos0 + r0 + SUBLANES * i + lax.broadcasted_iota(jnp.int32, cur[0].shape, 0)
            cnt = jnp.minimum(pos + 1, w).astype(F32)
            d.append(s[n_ctx + i] / cnt - cur[n_ctx + i])
        d_ref[r0:r0 + ROW_CHUNK, sl] = jnp.concatenate(d, axis=0).astype(BF16)


def _mix_prompt_kernel(x_ref, g_ref, w_ref, cw_ref, cb_ref, lg_ref, lb_ref, pw_ref, ps_ref,
                       a_ref, ctail_ref, ptail_ref, vext_ref, uext_ref, d_ref, wb_ref, *, tm, tiles_per_seq):
    s = pl.program_id(0)
    n_tiles = pl.num_programs(0) - 1
    slot = lax.rem(s, 2)
    v_new, u_new = vext_ref.at[slot], uext_ref.at[slot]
    v_old, u_old = vext_ref.at[1 - slot], uext_ref.at[1 - slot]

    @pl.when(s == 0)
    def _():
        vext_ref[1] = jnp.zeros(vext_ref.shape[1:], F32)
        uext_ref[1] = jnp.zeros(uext_ref.shape[1:], F32)
        for k in range(CONV_WIDTH):
            wb_ref[k] = jnp.broadcast_to(cw_ref[k:k + 1, :], wb_ref.shape[1:])

    h = _rms(x_ref[...], g_ref[...]).astype(BF16)
    nc = 512
    for c in range(CONV_DIM // nc):
        sl = slice(c * nc, (c + 1) * nc)
        a_val = jnp.dot(h, w_ref[:, sl], preferred_element_type=F32)
        a_gate = jnp.dot(h, w_ref[:, CONV_DIM + c * nc:CONV_DIM + (c + 1) * nc], preferred_element_type=F32)
        v_new[CONV_HALO:, sl] = a_val * jax.nn.sigmoid(a_gate)
        u_new[POOL_HALO:, sl] = jnp.dot(h, w_ref[:, 2 * CONV_DIM + c * nc:2 * CONV_DIM + (c + 1) * nc],
                                       preferred_element_type=F32)
    starts_seq = lax.rem(jnp.minimum(s, n_tiles - 1), tiles_per_seq) == 0
    v_new[0:CONV_HALO, :] = jnp.where(starts_seq, 0.0, v_old[tm:tm + CONV_HALO, :])
    u_new[0:POOL_HALO, :] = jnp.where(starts_seq, 0.0, u_old[tm:tm + POOL_HALO, :])

    pos0 = lax.rem(jnp.maximum(s - 1, 0), tiles_per_seq) * tm
    cb, lg, lb = cb_ref[...], lg_ref[...], lb_ref[...]
    for r in range(tm // ROW_CHUNK):
        r0 = r * ROW_CHUNK
        conv = jnp.concatenate(
            [_conv_chunk(v_old, wb_ref, r0, slice(c * LANE_GROUP, (c + 1) * LANE_GROUP))
             for c in range(CONV_DIM // LANE_GROUP)], axis=1)
        a_ref[r0:r0 + ROW_CHUNK, 0:CONV_DIM] = _conv_post(conv, cb, lg, lb).astype(BF16)
        _pool_chunk(u_old, r0, pos0, d_ref)
    for g in range(len(POOL_WINDOWS)):
        sl = slice(g * POOL_GROUP_DIM, (g + 1) * POOL_GROUP_DIM)
        y = jnp.dot(d_ref[:, sl], pw_ref[g], preferred_element_type=F32) * ps_ref[:, sl]
        a_ref[:, CONV_DIM + g * POOL_GROUP_DIM:CONV_DIM + (g + 1) * POOL_GROUP_DIM] = y.astype(BF16)
    ctail_ref[0] = v_old[tm:tm + CONV_HALO, :]
    ptail_ref[0] = u_old[tm:tm + POOL_HALO, :]


def _mix_prompt(x, g, w_in, cw, cb, lg, lb, pw, ps, *, batch, seq, tm):
    tiles_per_seq = seq // tm
    n_tiles = batch * tiles_per_seq
    cur = lambda s: (jnp.minimum(s, n_tiles - 1), 0)
    prev = lambda s: (jnp.maximum(s - 1, 0), 0)
    prev_seq = lambda s: (jnp.maximum(s - 1, 0) // tiles_per_seq, 0, 0)
    return pl.pallas_call(
        functools.partial(_mix_prompt_kernel, tm=tm, tiles_per_seq=tiles_per_seq),
        out_shape=(jax.ShapeDtypeStruct((batch * seq, D_MODEL), BF16),
                   jax.ShapeDtypeStruct((batch, CONV_HALO, CONV_DIM), F32),
                   jax.ShapeDtypeStruct((batch, POOL_HALO, POOL_DIM), F32)),
        grid=(n_tiles + 1,),
        in_specs=[
            pl.BlockSpec((tm, D_MODEL), cur),
            _resident(g.shape), _resident(w_in.shape), _resident(cw.shape), _resident(cb.shape),
            _resident(lg.shape), _resident(lb.shape), _resident(pw.shape), _resident(ps.shape),
        ],
        out_specs=(pl.BlockSpec((tm, D_MODEL), prev),
                   pl.BlockSpec((1, CONV_HALO, CONV_DIM), prev_seq),
                   pl.BlockSpec((1, POOL_HALO, POOL_DIM), prev_seq)),
        scratch_shapes=[
            pltpu.VMEM((2, CONV_HALO + tm, CONV_DIM), F32),
            pltpu.VMEM((2, POOL_HALO + tm, POOL_DIM), F32),
            pltpu.VMEM((tm, POOL_DIM), BF16),
            pltpu.VMEM((CONV_WIDTH, SUBLANES, CONV_DIM), F32),
        ],
        compiler_params=_params(("arbitrary",)),
        name="mix_prompt",
    )(x, g, w_in, cw, cb, lg, lb, pw, ps)


def _out_proj_kernel(x_ref, a_ref, wo_ref, o_ref):
    o_ref[...] = x_ref[...] + jnp.dot(a_ref[...], wo_ref[...], preferred_element_type=F32)


def _out_proj(x, a, wo, *, tm):
    rows = x.shape[0]
    return pl.pallas_call(
        _out_proj_kernel,
        out_shape=jax.ShapeDtypeStruct((rows, D_MODEL), F32),
        grid=(rows // tm,),
        in_specs=[pl.BlockSpec((tm, D_MODEL), lambda i: (i, 0)),
                  pl.BlockSpec((tm, D_MODEL), lambda i: (i, 0)),
                  _resident(wo.shape)],
        out_specs=pl.BlockSpec((tm, D_MODEL), lambda i: (i, 0)),
        compiler_params=_params(("parallel",)),
        name="out_proj",
    )(x, a, wo)


def _mix_out_sample_kernel(v_ref, u_ref, sc_ref, sp_ref, x_ref, cw_ref, cb_ref, lg_ref, lb_ref,
                           pw_ref, ps_ref, wo_ref, o_ref, acc_ref, s_ref, a_ref, d_ref, *, nb):
    cw_ctx = cw_ref[0:CONV_CTX, :]
    rowi = lax.broadcasted_iota(jnp.int32, (POOL_CTX, POOL_DIM), 0)
    lane = lax.broadcasted_iota(jnp.int32, (POOL_CTX, POOL_DIM), 1)
    win = jnp.zeros((POOL_CTX, POOL_DIM), jnp.int32)
    for g, w in enumerate(POOL_WINDOWS):
        win = jnp.where(lane // POOL_GROUP_DIM == g, w, win)
    in_window = (rowi >= POOL_CTX + 1 - win).astype(F32)

    def per_seq(b, carry):
        acc_ref[pl.ds(b, 1), :] = jnp.sum(sc_ref[b] * cw_ctx, axis=0, keepdims=True)
        s_ref[pl.ds(b, 1), :] = jnp.sum(sp_ref[b] * in_window, axis=0, keepdims=True)
        return carry

    lax.fori_loop(0, nb, per_seq, 0)

    acc = acc_ref[...] + cw_ref[CONV_CTX:CONV_WIDTH, :] * v_ref[...]
    a_ref[:, 0:CONV_DIM] = _conv_post(acc, cb_ref[...], lg_ref[...], lb_ref[...]).astype(BF16)

    u = u_ref[...]
    s = s_ref[...] + u
    lane1 = lax.broadcasted_iota(jnp.int32, (1, POOL_DIM), 1)
    cnt = jnp.zeros((1, POOL_DIM), F32)
    for g, w in enumerate(POOL_WINDOWS):
        cnt = jnp.where(lane1 // POOL_GROUP_DIM == g, float(min(PAST_LEN + 1, w)), cnt)
    d_ref[...] = (s / cnt - u).astype(BF16)
    _mix_tail(a_ref, d_ref, x_ref, pw_ref, ps_ref, wo_ref, o_ref)


def _mix_out_sample(v, u, sc, sp, x, cw, cb, lg, lb, pw, ps, wo):
    nb = v.shape[0]
    args = (v, u, sc, sp, x, cw, cb, lg, lb, pw, ps, wo)
    return pl.pallas_call(
        functools.partial(_mix_out_sample_kernel, nb=nb),
        out_shape=jax.ShapeDtypeStruct((nb, D_MODEL), F32),
        grid=(1,),
        in_specs=[_resident(a.shape) for a in args],
        out_specs=pl.BlockSpec((nb, D_MODEL), lambda i: (0, 0)),
        scratch_shapes=[
            pltpu.VMEM((nb, CONV_DIM), F32),
            pltpu.VMEM((nb, POOL_DIM), F32),
            pltpu.VMEM((nb, D_MODEL), BF16),
            pltpu.VMEM((nb, POOL_DIM), BF16),
        ],
        compiler_params=_params(("arbitrary",)),
        name="mix_out_sample",
    )(*args)


def _ple_kernel(x_ref, p_ref, gp_ref, wg_ref, wp_ref, gf_ref, o_ref):
    x = x_ref[...]
    r = _rms(x, gp_ref[...]).astype(BF16)
    p = p_ref[...].astype(BF16)
    nc = 512
    for c in range(D_MODEL // nc):
        sl = slice(c * nc, (c + 1) * nc)
        gate = jax.nn.sigmoid(jnp.dot(r, wg_ref[:, sl], preferred_element_type=F32))
        proj = jnp.dot(p, wp_ref[:, sl], preferred_element_type=F32)
        o_ref[:, sl] = x_ref[:, sl] + gate * proj
    o_ref[...] = _rms(o_ref[...], gf_ref[...])


def _ple(x, p, gp, wg, wp, gf, *, tm):
    rows = x.shape[0]
    return pl.pallas_call(
        _ple_kernel,
        out_shape=jax.ShapeDtypeStruct((rows, D_MODEL), F32),
        grid=(rows // tm,),
        in_specs=[
            pl.BlockSpec((tm, D_MODEL), lambda i: (i, 0)),
            pl.BlockSpec((tm, PLE_DIM), lambda i: (i, 0)),
            _resident((1, D_MODEL)), _resident(wg.shape), _resident(wp.shape), _resident((1, D_MODEL)),
        ],
        out_specs=pl.BlockSpec((tm, D_MODEL), lambda i: (i, 0)),
        compiler_params=_params(("parallel",)),
        name="ple",
    )(x, p, gp, wg, wp, gf)


def kernel(x_prompt, x_sample, state_conv, state_pool, p_prompt, p_sample, norm_ffn1, w_ffn1_in, w_ffn1_out, norm_mix, w_in, conv_w, conv_b, conv_ln_g, conv_ln_b, pool_w, pool_scale, w_out, norm_ffn2, w_ffn2_in, w_ffn2_out, norm_ple, w_ple_gate, w_ple_proj, norm_final):
    assert norm_ffn1.shape[0] == 1, "the final norm is fused into the layer's last stage: one layer only"
    batch, seq, _ = x_prompt.shape
    nb = x_sample.shape[0]
    xp = x_prompt.reshape(batch * seq, D_MODEL)
    xs = x_sample.reshape(nb, D_MODEL)
    row = lambda a: a.reshape(1, -1)

    wi, wo = w_in[0].astype(BF16), w_out[0].astype(BF16)
    pw, wg, wp = pool_w[0].astype(BF16), w_ple_gate[0].astype(BF16), w_ple_proj[0].astype(BF16)
    g1, gm, g2, gp, gf = row(norm_ffn1[0]), row(norm_mix[0]), row(norm_ffn2[0]), row(norm_ple[0]), row(norm_final)
    mix = (conv_w[0], row(conv_b[0]), row(conv_ln_g[0]), row(conv_ln_b[0]), pw, row(pool_scale[0]), wo)

    x1, x1s = _ffn(xp, xs, g1, w_ffn1_in[0], w_ffn1_out[0], tm=FFN_TM, tf=FFN_TF)
    a, v_tail, u_tail = _mix_prompt(x1, gm, wi, *mix[:-1], batch=batch, seq=seq, tm=PROMPT_TM)
    x2 = _out_proj(x1, a, wo, tm=PROMPT_TM)
    vs, us = _mix_in(x1s, gm, wi, tm=nb)
    x2s = _mix_out_sample(vs, us, state_conv[0], state_pool[0], x1s, *mix)
    x3, x3s = _ffn(x2, x2s, g2, w_ffn2_in[0], w_ffn2_out[0], tm=FFN_TM, tf=FFN_TF)
    yp = _ple(x3, p_prompt[0].reshape(batch * seq, PLE_DIM), gp, wg, wp, gf, tm=PROMPT_TM)
    ys = _ple(x3s, p_sample[0].reshape(nb, PLE_DIM), gp, wg, wp, gf, tm=nb)
    new_conv_p = v_tail[:, CONV_HALO - CONV_CTX:]
    new_pool_p = u_tail[:, POOL_HALO - POOL_CTX:]
    new_conv_s = jnp.concatenate([state_conv[0][:, 1:], vs[:, None, :]], axis=1)
    new_pool_s = jnp.concatenate([state_pool[0][:, 1:], us[:, None, :]], axis=1)

    return (yp.reshape(batch, seq, D_MODEL), ys.reshape(nb, 1, D_MODEL),
            new_conv_p[None], new_conv_s[None], new_pool_p[None], new_pool_s[None])
```

```python
import functools

import jax
import jax.numpy as jnp
from jax import lax
from jax.experimental import pallas as pl
from jax.experimental.pallas import tpu as pltpu

D_MODEL = 2048
D_FF = 5632
CONV_DIM = 1024
POOL_DIM = 1024
POOL_WINDOWS = (2, 4, 8, 16)
POOL_GROUP_DIM = POOL_DIM // len(POOL_WINDOWS)
POOL_CTX = max(POOL_WINDOWS) - 1
CONV_WIDTH = 31
CONV_CTX = CONV_WIDTH - 1
PLE_DIM = 256
PAST_LEN = 16384
EPS = 1e-6

F32 = jnp.float32
BF16 = jnp.bfloat16

VMEM_LIMIT_BYTES = 62 * 1024 * 1024
PROMPT_TM = 512
FFN_TM = 1024
FFN_TF = 512
CONV_HALO = 32
POOL_HALO = 16
ROW_CHUNK = 64
LANE_GROUP = 256
SAMPLE_BLOCK = 16
SIDE_BLOCKS = 16
SUBLANES = 8


def _rms(x, g):
    ms = jnp.mean(x * x, axis=-1, keepdims=True)
    return x * lax.rsqrt(ms + EPS) * g


def _params(sem):
    return pltpu.CompilerParams(dimension_semantics=sem, vmem_limit_bytes=VMEM_LIMIT_BYTES)


def _resident(shape):
    nd = len(shape)
    return pl.BlockSpec(shape, lambda *_: (0,) * nd, pipeline_mode=pl.Buffered(1))


def _ffn_rows(j, x_ref, g_ref, w_refs, o_ref, h_ref, bf16_copy_refs=(None, None, None)):
    @pl.when(j == 0)
    def _():
        x = x_ref[...]
        h_ref[...] = _rms(x, g_ref[...]).astype(BF16)
        o_ref[...] = x

    def weight(k):
        w = w_refs[k][...].astype(BF16)
        if bf16_copy_refs[k] is not None:
            bf16_copy_refs[k][...] = w
        return w

    h = h_ref[...]
    gate = jnp.dot(h, weight(0), preferred_element_type=F32)
    up = jnp.dot(h, weight(1), preferred_element_type=F32)
    act = (gate * jax.nn.sigmoid(gate) * up * 0.5).astype(BF16)
    o_ref[...] += jnp.dot(act, weight(2), preferred_element_type=F32)


def _ffn_head_kernel(*refs, n_side):
    x_ref, g_ref, wg_ref, wu_ref, wo_ref = refs[:5]
    side_in = refs[5:5 + n_side]
    o_ref, wgb_ref, wub_ref, wob_ref = refs[5 + n_side:9 + n_side]
    side_out = refs[9 + n_side:9 + 2 * n_side]
    h_ref = refs[9 + 2 * n_side]
    for src, dst in zip(side_in, side_out):
        dst[...] = src[...].astype(BF16)
    _ffn_rows(pl.program_id(0), x_ref, g_ref, (wg_ref, wu_ref, wo_ref), o_ref, h_ref,
              bf16_copy_refs=(wgb_ref, wub_ref, wob_ref))


def _ffn_body_kernel(head_ref, xp_ref, xs_ref, g_ref, wg_ref, wu_ref, wo_ref, op_ref, os_ref, h_ref, *, tm):
    del head_ref
    i, j = pl.program_id(0), pl.program_id(1)
    ns = xs_ref.shape[0]
    w_refs = (wg_ref, wu_ref, wo_ref)
    _ffn_rows(j, xp_ref, g_ref, w_refs, op_ref, h_ref.at[0:tm])

    @pl.when(i == pl.num_programs(0) - 1)
    def _():
        _ffn_rows(j, xs_ref, g_ref, w_refs, os_ref, h_ref.at[tm:tm + ns])


def _ffn(xp, xs, g, w_in, w_out, *, tm, tf, side=()):
    ns = xs.shape[0]
    nf = D_FF // tf
    hf = tf // 2
    nh = D_FF // hf
    once = dict(pipeline_mode=pl.Buffered(1))
    side_rows = [a.shape[0] // SIDE_BLOCKS for a in side]
    side_block = lambda j: (jnp.minimum(j, SIDE_BLOCKS - 1), 0)
    side_specs = [pl.BlockSpec((r, a.shape[1]), side_block) for r, a in zip(side_rows, side)]
    head, wg, wu, wo, *side_bf16 = pl.pallas_call(
        functools.partial(_ffn_head_kernel, n_side=len(side)),
        out_shape=(jax.ShapeDtypeStruct(xp.shape, F32),
                   jax.ShapeDtypeStruct((D_MODEL, D_FF), BF16),
                   jax.ShapeDtypeStruct((D_MODEL, D_FF), BF16),
                   jax.ShapeDtypeStruct((D_FF, D_MODEL), BF16),
                   *[jax.ShapeDtypeStruct(a.shape, BF16) for a in side]),
        grid=(nh,),
        in_specs=[
            pl.BlockSpec((tm, D_MODEL), lambda j: (0, 0), **once),
            pl.BlockSpec((1, D_MODEL), lambda j: (0, 0), **once),
            pl.BlockSpec((D_MODEL, hf), lambda j: (0, j)),
            pl.BlockSpec((D_MODEL, hf), lambda j: (0, j + nh)),
            pl.BlockSpec((hf, D_MODEL), lambda j: (j, 0)),
            *side_specs,
        ],
        out_specs=(pl.BlockSpec((tm, D_MODEL), lambda j: (0, 0)),
                   pl.BlockSpec((D_MODEL, hf), lambda j: (0, j)),
                   pl.BlockSpec((D_MODEL, hf), lambda j: (0, j)),
                   pl.BlockSpec((hf, D_MODEL), lambda j: (j, 0)),
                   *side_specs),
        scratch_shapes=[pltpu.VMEM((tm, D_MODEL), BF16)],
        compiler_params=_params(("arbitrary",)),
        name="ffn_head",
    )(xp, g, w_in, w_in, w_out, *side)
    out_p, out_s = pl.pallas_call(
        functools.partial(_ffn_body_kernel, tm=tm),
        out_shape=(jax.ShapeDtypeStruct(xp.shape, F32), jax.ShapeDtypeStruct(xs.shape, F32)),
        grid=(xp.shape[0] // tm - 1, nf),
        in_specs=[
            pl.BlockSpec(memory_space=pl.ANY),
            pl.BlockSpec((tm, D_MODEL), lambda i, j: (i + 1, 0)),
            pl.BlockSpec((ns, D_MODEL), lambda i, j: (0, 0), **once),
            pl.BlockSpec((1, D_MODEL), lambda i, j: (0, 0), **once),
            pl.BlockSpec((D_MODEL, tf), lambda i, j: (0, j)),
            pl.BlockSpec((D_MODEL, tf), lambda i, j: (0, j)),
            pl.BlockSpec((tf, D_MODEL), lambda i, j: (j, 0)),
        ],
        out_specs=(pl.BlockSpec((tm, D_MODEL), lambda i, j: (i + 1, 0)),
                   pl.BlockSpec((ns, D_MODEL), lambda i, j: (0, 0))),
        scratch_shapes=[pltpu.VMEM((tm + ns, D_MODEL), BF16)],
        input_output_aliases={0: 0},
        compiler_params=_params(("arbitrary", "arbitrary")),
        name="ffn_body",
    )(head, xp, xs, g, wg, wu, wo)
    return out_p, out_s, side_bf16


def _mix_in_kernel(x_ref, g_ref, w_ref, v_ref, u_ref):
    h = _rms(x_ref[...], g_ref[...]).astype(BF16)
    nc = 512
    for c in range(CONV_DIM // nc):
        a_val = jnp.dot(h, w_ref[:, c * nc:(c + 1) * nc], preferred_element_type=F32)
        a_gate = jnp.dot(h, w_ref[:, CONV_DIM + c * nc:CONV_DIM + (c + 1) * nc],
                         preferred_element_type=F32)
        v_ref[:, c * nc:(c + 1) * nc] = a_val * jax.nn.sigmoid(a_gate)
        u_ref[:, c * nc:(c + 1) * nc] = jnp.dot(
            h, w_ref[:, 2 * CONV_DIM + c * nc:2 * CONV_DIM + (c + 1) * nc],
            preferred_element_type=F32)


def _mix_in(x, g, w_in, *, tm):
    rows = x.shape[0]
    return pl.pallas_call(
        _mix_in_kernel,
        out_shape=(jax.ShapeDtypeStruct((rows, CONV_DIM), F32),
                   jax.ShapeDtypeStruct((rows, POOL_DIM), F32)),
        grid=(rows // tm,),
        in_specs=[
            pl.BlockSpec((tm, D_MODEL), lambda i: (i, 0)),
            _resident((1, D_MODEL)),
            _resident(w_in.shape),
        ],
        out_specs=(pl.BlockSpec((tm, CONV_DIM), lambda i: (i, 0)),
                   pl.BlockSpec((tm, POOL_DIM), lambda i: (i, 0))),
        compiler_params=_params(("parallel",)),
        name="mix_in",
    )(x, g, w_in)


def _conv_post(acc, cb, lg, lb):
    y = acc + cb
    mu = jnp.mean(y, axis=-1, keepdims=True)
    yc = y - mu
    var = jnp.mean(yc * yc, axis=-1, keepdims=True)
    z = yc * lax.rsqrt(var + EPS) * lg + lb
    return z * jax.nn.sigmoid(z)


def _mix_tail(a_ref, d_ref, x_ref, pw_ref, ps_ref, wo_ref, o_ref):
    for g in range(len(POOL_WINDOWS)):
        sl = slice(g * POOL_GROUP_DIM, (g + 1) * POOL_GROUP_DIM)
        y = jnp.dot(d_ref[:, sl], pw_ref[g], preferred_element_type=F32) * ps_ref[:, sl]
        a_ref[:, CONV_DIM + g * POOL_GROUP_DIM:CONV_DIM + (g + 1) * POOL_GROUP_DIM] = y.astype(BF16)
    o_ref[...] = x_ref[...] + jnp.dot(a_ref[...], wo_ref[...], preferred_element_type=F32)


def _rows_above(tiles, b):
    rot = [pltpu.roll(t, SUBLANES - b, axis=0) for t in tiles]
    own = lax.broadcasted_iota(jnp.int32, tiles[0].shape, 0) < SUBLANES - b
    return [jnp.where(own, rot[j], rot[j + 1]) for j in range(len(tiles) - 1)]


def _rows_below(tiles, d):
    rot = [pltpu.roll(t, d, axis=0) for t in tiles]
    own = lax.broadcasted_iota(jnp.int32, tiles[0].shape, 0) >= d
    return [jnp.where(own, rot[j], rot[max(j - 1, 0)]) for j in range(len(tiles))]


def _conv_chunk(v_ref, wb_ref, r0, sl):
    n_out = ROW_CHUNK // SUBLANES
    n_src = n_out + CONV_HALO // SUBLANES
    src = [v_ref[r0 + SUBLANES * j:r0 + SUBLANES * (j + 1), sl] for j in range(n_src)]
    acc = [None] * n_out
    for b in range(SUBLANES):
        taps = [k for k in range(CONV_WIDTH) if (CONV_HALO - CONV_CTX + k) % SUBLANES == b]
        if not taps:
            continue
        shifted = src if b == 0 else _rows_above(src, b)
        for k in taps:
            a = (CONV_HALO - CONV_CTX + k) // SUBLANES
            wk = wb_ref[k, :, sl]
            for i in range(n_out):
                term = wk * shifted[a + i]
                acc[i] = term if acc[i] is None else acc[i] + term
    return jnp.concatenate(acc, axis=0)


def _pool_chunk(u_ref, r0, pos0, d_ref):
    n_out = ROW_CHUNK // SUBLANES
    n_ctx = POOL_HALO // SUBLANES
    for g, w in enumerate(POOL_WINDOWS):
        sl = slice(g * POOL_GROUP_DIM, (g + 1) * POOL_GROUP_DIM)
        cur = [u_ref[r0 + SUBLANES * j:r0 + SUBLANES * (j + 1), sl] for j in range(n_out + n_ctx)]
        s, span = cur, 1
        while span < w:
            if span < SUBLANES:
                below = _rows_below(s, span)
            else:
                below = [s[0]] + s[:-1]
            s = [x + y for x, y in zip(s, below)]
            span *= 2
        d = []
        for i in range(n_out):
            pos = pos0 + r0 + SUBLANES * i + lax.broadcasted_iota(jnp.int32, cur[0].shape, 0)
            cnt = jnp.minimum(pos + 1, w).astype(F32)
            d.append(s[n_ctx + i] / cnt - cur[n_ctx + i])
        d_ref[r0:r0 + ROW_CHUNK, sl] = jnp.concatenate(d, axis=0).astype(BF16)


def _mix_prompt_kernel(x_ref, g_ref, w_ref, cw_ref, cb_ref, lg_ref, lb_ref, pw_ref, ps_ref,
                       a_ref, ctail_ref, ptail_ref, vext_ref, uext_ref, d_ref, wb_ref, *, tm, tiles_per_seq):
    s = pl.program_id(0)
    n_tiles = pl.num_programs(0) - 1
    slot = lax.rem(s, 2)
    v_new, u_new = vext_ref.at[slot], uext_ref.at[slot]
    v_old, u_old = vext_ref.at[1 - slot], uext_ref.at[1 - slot]

    @pl.when(s == 0)
    def _():
        vext_ref[1] = jnp.zeros(vext_ref.shape[1:], F32)
        uext_ref[1] = jnp.zeros(uext_ref.shape[1:], F32)
        for k in range(CONV_WIDTH):
            wb_ref[k] = jnp.broadcast_to(cw_ref[k:k + 1, :], wb_ref.shape[1:])

    h = _rms(x_ref[...], g_ref[...]).astype(BF16)
    nc = 512
    for c in range(CONV_DIM // nc):
        sl = slice(c * nc, (c + 1) * nc)
        a_val = jnp.dot(h, w_ref[:, sl], preferred_element_type=F32)
        a_gate = jnp.dot(h, w_ref[:, CONV_DIM + c * nc:CONV_DIM + (c + 1) * nc], preferred_element_type=F32)
        v_new[CONV_HALO:, sl] = a_val * jax.nn.sigmoid(a_gate)
        u_new[POOL_HALO:, sl] = jnp.dot(h, w_ref[:, 2 * CONV_DIM + c * nc:2 * CONV_DIM + (c + 1) * nc],
                                       preferred_element_type=F32)
    starts_seq = lax.rem(jnp.minimum(s, n_tiles - 1), tiles_per_seq) == 0
    v_new[0:CONV_HALO, :] = jnp.where(starts_seq, 0.0, v_old[tm:tm + CONV_HALO, :])
    u_new[0:POOL_HALO, :] = jnp.where(starts_seq, 0.0, u_old[tm:tm + POOL_HALO, :])

    pos0 = lax.rem(jnp.maximum(s - 1, 0), tiles_per_seq) * tm
    cb, lg, lb = cb_ref[...], lg_ref[...], lb_ref[...]
    for r in range(tm // ROW_CHUNK):
        r0 = r * ROW_CHUNK
        conv = jnp.concatenate(
            [_conv_chunk(v_old, wb_ref, r0, slice(c * LANE_GROUP, (c + 1) * LANE_GROUP))
             for c in range(CONV_DIM // LANE_GROUP)], axis=1)
        a_ref[r0:r0 + ROW_CHUNK, 0:CONV_DIM] = _conv_post(conv, cb, lg, lb).astype(BF16)
        _pool_chunk(u_old, r0, pos0, d_ref)
    for g in range(len(POOL_WINDOWS)):
        sl = slice(g * POOL_GROUP_DIM, (g + 1) * POOL_GROUP_DIM)
        y = jnp.dot(d_ref[:, sl], pw_ref[g], preferred_element_type=F32) * ps_ref[:, sl]
        a_ref[:, CONV_DIM + g * POOL_GROUP_DIM:CONV_DIM + (g + 1) * POOL_GROUP_DIM] = y.astype(BF16)
    ctail_ref[0] = v_old[tm:tm + CONV_HALO, :]
    ptail_ref[0] = u_old[tm:tm + POOL_HALO, :]


def _mix_prompt(x, g, w_in, cw, cb, lg, lb, pw, ps, *, batch, seq, tm):
    tiles_per_seq = seq // tm
    n_tiles = batch * tiles_per_seq
    cur = lambda s: (jnp.minimum(s, n_tiles - 1), 0)
    prev = lambda s: (jnp.maximum(s - 1, 0), 0)
    prev_seq = lambda s: (jnp.maximum(s - 1, 0) // tiles_per_seq, 0, 0)
    return pl.pallas_call(
        functools.partial(_mix_prompt_kernel, tm=tm, tiles_per_seq=tiles_per_seq),
        out_shape=(jax.ShapeDtypeStruct((batch * seq, D_MODEL), BF16),
                   jax.ShapeDtypeStruct((batch, CONV_HALO, CONV_DIM), F32),
                   jax.ShapeDtypeStruct((batch, POOL_HALO, POOL_DIM), F32)),
        grid=(n_tiles + 1,),
        in_specs=[
            pl.BlockSpec((tm, D_MODEL), cur),
            _resident(g.shape), _resident(w_in.shape), _resident(cw.shape), _resident(cb.shape),
            _resident(lg.shape), _resident(lb.shape), _resident(pw.shape), _resident(ps.shape),
        ],
        out_specs=(pl.BlockSpec((tm, D_MODEL), prev),
                   pl.BlockSpec((1, CONV_HALO, CONV_DIM), prev_seq),
                   pl.BlockSpec((1, POOL_HALO, POOL_DIM), prev_seq)),
        scratch_shapes=[
            pltpu.VMEM((2, CONV_HALO + tm, CONV_DIM), F32),
            pltpu.VMEM((2, POOL_HALO + tm, POOL_DIM), F32),
            pltpu.VMEM((tm, POOL_DIM), BF16),
            pltpu.VMEM((CONV_WIDTH, SUBLANES, CONV_DIM), F32),
        ],
        compiler_params=_params(("arbitrary",)),
        name="mix_prompt",
    )(x, g, w_in, cw, cb, lg, lb, pw, ps)


def _out_proj_kernel(x_ref, a_ref, wo_ref, o_ref):
    o_ref[...] = x_ref[...] + jnp.dot(a_ref[...], wo_ref[...], preferred_element_type=F32)


def _out_proj(x, a, wo, *, tm):
    rows = x.shape[0]
    return pl.pallas_call(
        _out_proj_kernel,
        out_shape=jax.ShapeDtypeStruct((rows, D_MODEL), F32),
        grid=(rows // tm,),
        in_specs=[pl.BlockSpec((tm, D_MODEL), lambda i: (i, 0)),
                  pl.BlockSpec((tm, D_MODEL), lambda i: (i, 0)),
                  _resident(wo.shape)],
        out_specs=pl.BlockSpec((tm, D_MODEL), lambda i: (i, 0)),
        compiler_params=_params(("parallel",)),
        name="out_proj",
    )(x, a, wo)


def _mix_out_sample_kernel(v_ref, u_ref, sc_ref, sp_ref, x_ref, cw_ref, cb_ref, lg_ref, lb_ref,
                           pw_ref, ps_ref, wo_ref, o_ref, nsc_ref, nsp_ref, acc_ref, s_ref, a_ref, d_ref):
    c = pl.program_id(0)
    sb = sc_ref.shape[0]
    cw_ctx = cw_ref[0:CONV_CTX, :]
    rowi = lax.broadcasted_iota(jnp.int32, (POOL_CTX, POOL_DIM), 0)
    lane = lax.broadcasted_iota(jnp.int32, (POOL_CTX, POOL_DIM), 1)
    win = jnp.zeros((POOL_CTX, POOL_DIM), jnp.int32)
    for g, w in enumerate(POOL_WINDOWS):
        win = jnp.where(lane // POOL_GROUP_DIM == g, w, win)
    in_window = (rowi >= POOL_CTX + 1 - win).astype(F32)

    def per_seq(b, carry):
        row = c * sb + b
        acc_ref[pl.ds(row, 1), :] = jnp.sum(sc_ref[b] * cw_ctx, axis=0, keepdims=True)
        s_ref[pl.ds(row, 1), :] = jnp.sum(sp_ref[b] * in_window, axis=0, keepdims=True)
        nsc_ref[b, 0:CONV_CTX - 1, :] = sc_ref[b, 1:CONV_CTX, :]
        nsc_ref[b, CONV_CTX - 1:CONV_CTX, :] = v_ref[pl.ds(row, 1), :]
        nsp_ref[b, 0:POOL_CTX - 1, :] = sp_ref[b, 1:POOL_CTX, :]
        nsp_ref[b, POOL_CTX - 1:POOL_CTX, :] = u_ref[pl.ds(row, 1), :]
        return carry

    lax.fori_loop(0, sb, per_seq, 0)

    @pl.when(c == pl.num_programs(0) - 1)
    def _():
        acc = acc_ref[...] + cw_ref[CONV_CTX:CONV_WIDTH, :] * v_ref[...]
        a_ref[:, 0:CONV_DIM] = _conv_post(acc, cb_ref[...], lg_ref[...], lb_ref[...]).astype(BF16)
        u = u_ref[...]
        s = s_ref[...] + u
        lane1 = lax.broadcasted_iota(jnp.int32, (1, POOL_DIM), 1)
        cnt = jnp.zeros((1, POOL_DIM), F32)
        for g, w in enumerate(POOL_WINDOWS):
            cnt = jnp.where(lane1 // POOL_GROUP_DIM == g, float(min(PAST_LEN + 1, w)), cnt)
        d_ref[...] = (s / cnt - u).astype(BF16)
        _mix_tail(a_ref, d_ref, x_ref, pw_ref, ps_ref, wo_ref, o_ref)


def _mix_out_sample(v, u, sc, sp, x, cw, cb, lg, lb, pw, ps, wo):
    nb = v.shape[0]
    whole = (v, u, x, cw, cb, lg, lb, pw, ps, wo)
    v_s, u_s, x_s, *w_s = [_resident(a.shape) for a in whole]
    seq_block = lambda a: pl.BlockSpec((SAMPLE_BLOCK,) + a.shape[1:], lambda c: (c, 0, 0))
    return pl.pallas_call(
        _mix_out_sample_kernel,
        out_shape=(jax.ShapeDtypeStruct((nb, D_MODEL), F32),
                   jax.ShapeDtypeStruct(sc.shape, F32), jax.ShapeDtypeStruct(sp.shape, F32)),
        grid=(nb // SAMPLE_BLOCK,),
        in_specs=[v_s, u_s, seq_block(sc), seq_block(sp), x_s, *w_s],
        out_specs=(pl.BlockSpec((nb, D_MODEL), lambda c: (0, 0)), seq_block(sc), seq_block(sp)),
        scratch_shapes=[
            pltpu.VMEM((nb, CONV_DIM), F32),
            pltpu.VMEM((nb, POOL_DIM), F32),
            pltpu.VMEM((nb, D_MODEL), BF16),
            pltpu.VMEM((nb, POOL_DIM), BF16),
        ],
        compiler_params=_params(("arbitrary",)),
        name="mix_out_sample",
    )(v, u, sc, sp, x, cw, cb, lg, lb, pw, ps, wo)


def _ple_kernel(x_ref, p_ref, gp_ref, wg_ref, wp_ref, gf_ref, o_ref):
    x = x_ref[...]
    r = _rms(x, gp_ref[...]).astype(BF16)
    p = p_ref[...].astype(BF16)
    nc = 512
    for c in range(D_MODEL // nc):
        sl = slice(c * nc, (c + 1) * nc)
        gate = jax.nn.sigmoid(jnp.dot(r, wg_ref[:, sl], preferred_element_type=F32))
        proj = jnp.dot(p, wp_ref[:, sl], preferred_element_type=F32)
        o_ref[:, sl] = x_ref[:, sl] + gate * proj
    o_ref[...] = _rms(o_ref[...], gf_ref[...])


def _ple(x, p, gp, wg, wp, gf, *, tm):
    rows = x.shape[0]
    return pl.pallas_call(
        _ple_kernel,
        out_shape=jax.ShapeDtypeStruct((rows, D_MODEL), F32),
        grid=(rows // tm,),
        in_specs=[
            pl.BlockSpec((tm, D_MODEL), lambda i: (i, 0)),
            pl.BlockSpec((tm, PLE_DIM), lambda i: (i, 0)),
            _resident((1, D_MODEL)), _resident(wg.shape), _resident(wp.shape), _resident((1, D_MODEL)),
        ],
        out_specs=pl.BlockSpec((tm, D_MODEL), lambda i: (i, 0)),
        compiler_params=_params(("parallel",)),
        name="ple",
    )(x, p, gp, wg, wp, gf)


def kernel(x_prompt, x_sample, state_conv, state_pool, p_prompt, p_sample, norm_ffn1, w_ffn1_in, w_ffn1_out, norm_mix, w_in, conv_w, conv_b, conv_ln_g, conv_ln_b, pool_w, pool_scale, w_out, norm_ffn2, w_ffn2_in, w_ffn2_out, norm_ple, w_ple_gate, w_ple_proj, norm_final):
    assert norm_ffn1.shape[0] == 1, "the final norm is fused into the layer's last stage: one layer only"
    batch, seq, _ = x_prompt.shape
    nb = x_sample.shape[0]
    xp = x_prompt.reshape(batch * seq, D_MODEL)
    xs = x_sample.reshape(nb, D_MODEL)
    row = lambda a: a.reshape(1, -1)

    g1, gm, g2, gp, gf = row(norm_ffn1[0]), row(norm_mix[0]), row(norm_ffn2[0]), row(norm_ple[0]), row(norm_final)
    pool_w2d = pool_w[0].reshape(len(POOL_WINDOWS) * POOL_GROUP_DIM, POOL_GROUP_DIM)

    x1, x1s, (wi, wo, pw) = _ffn(xp, xs, g1, w_ffn1_in[0], w_ffn1_out[0], tm=FFN_TM, tf=FFN_TF,
                                 side=(w_in[0], w_out[0], pool_w2d))
    pw = pw.reshape(pool_w[0].shape)
    mix = (conv_w[0], row(conv_b[0]), row(conv_ln_g[0]), row(conv_ln_b[0]), pw, row(pool_scale[0]), wo)
    a, v_tail, u_tail = _mix_prompt(x1, gm, wi, *mix[:-1], batch=batch, seq=seq, tm=PROMPT_TM)
    x2 = _out_proj(x1, a, wo, tm=PROMPT_TM)
    vs, us = _mix_in(x1s, gm, wi, tm=nb)
    x2s, new_conv_s, new_pool_s = _mix_out_sample(vs, us, state_conv[0], state_pool[0], x1s, *mix)
    x3, x3s, (wg, wp) = _ffn(x2, x2s, g2, w_ffn2_in[0], w_ffn2_out[0], tm=FFN_TM, tf=FFN_TF,
                             side=(w_ple_gate[0], w_ple_proj[0]))
    yp = _ple(x3, p_prompt[0].reshape(batch * seq, PLE_DIM), gp, wg, wp, gf, tm=PROMPT_TM)
    ys = _ple(x3s, p_sample[0].reshape(nb, PLE_DIM), gp, wg, wp, gf, tm=nb)
    new_conv_p = v_tail[:, CONV_HALO - CONV_CTX:]
    new_pool_p = u_tail[:, POOL_HALO - POOL_CTX:]

    return (yp.reshape(batch, seq, D_MODEL), ys.reshape(nb, 1, D_MODEL),
            new_conv_p[None], new_conv_s[None], new_pool_p[None], new_pool_s[None])
```

```python
import functools

import jax
import jax.numpy as jnp
from jax import lax
from jax.experimental import pallas as pl
from jax.experimental.pallas import tpu as pltpu

D_MODEL = 2048
D_FF = 5632
CONV_DIM = 1024
POOL_DIM = 1024
POOL_WINDOWS = (2, 4, 8, 16)
POOL_GROUP_DIM = POOL_DIM // len(POOL_WINDOWS)
POOL_CTX = max(POOL_WINDOWS) - 1
CONV_WIDTH = 31
CONV_CTX = CONV_WIDTH - 1
PLE_DIM = 256
PAST_LEN = 16384
EPS = 1e-6

F32 = jnp.float32
BF16 = jnp.bfloat16

VMEM_LIMIT_BYTES = 62 * 1024 * 1024
PROMPT_TM = 512
FFN_TM = 1024
FFN_TF = 512
CONV_HALO = 32
POOL_HALO = 16
ROW_CHUNK = 64
LANE_GROUP = 256
CONV_STEP_ROWS = CONV_CTX // POOL_CTX
SIDE_BLOCKS = 16
SUBLANES = 8


def _rms(x, g):
    ms = jnp.mean(x * x, axis=-1, keepdims=True)
    return x * lax.rsqrt(ms + EPS) * g


def _params(sem):
    return pltpu.CompilerParams(dimension_semantics=sem, vmem_limit_bytes=VMEM_LIMIT_BYTES)


def _resident(shape):
    nd = len(shape)
    return pl.BlockSpec(shape, lambda *_: (0,) * nd, pipeline_mode=pl.Buffered(1))


def _ffn_rows(j, x_ref, g_ref, w_refs, o_ref, h_ref, bf16_copy_refs=(None, None, None)):
    @pl.when(j == 0)
    def _():
        x = x_ref[...]
        h_ref[...] = _rms(x, g_ref[...]).astype(BF16)
        o_ref[...] = x

    def weight(k):
        w = w_refs[k][...].astype(BF16)
        if bf16_copy_refs[k] is not None:
            bf16_copy_refs[k][...] = w
        return w

    h = h_ref[...]
    gate = jnp.dot(h, weight(0), preferred_element_type=F32)
    up = jnp.dot(h, weight(1), preferred_element_type=F32)
    act = (gate * jax.nn.sigmoid(gate) * up * 0.5).astype(BF16)
    o_ref[...] += jnp.dot(act, weight(2), preferred_element_type=F32)


def _ffn_head_kernel(*refs, n_side):
    x_ref, g_ref, wg_ref, wu_ref, wo_ref = refs[:5]
    side_in = refs[5:5 + n_side]
    o_ref, wgb_ref, wub_ref, wob_ref = refs[5 + n_side:9 + n_side]
    side_out = refs[9 + n_side:9 + 2 * n_side]
    h_ref = refs[9 + 2 * n_side]
    for src, dst in zip(side_in, side_out):
        dst[...] = src[...].astype(BF16)
    _ffn_rows(pl.program_id(0), x_ref, g_ref, (wg_ref, wu_ref, wo_ref), o_ref, h_ref,
              bf16_copy_refs=(wgb_ref, wub_ref, wob_ref))


def _ffn_body_kernel(head_ref, xp_ref, xs_ref, g_ref, wg_ref, wu_ref, wo_ref, op_ref, os_ref, h_ref, *, tm):
    del head_ref
    i, j = pl.program_id(0), pl.program_id(1)
    ns = xs_ref.shape[0]
    w_refs = (wg_ref, wu_ref, wo_ref)
    _ffn_rows(j, xp_ref, g_ref, w_refs, op_ref, h_ref.at[0:tm])

    @pl.when(i == pl.num_programs(0) - 1)
    def _():
        _ffn_rows(j, xs_ref, g_ref, w_refs, os_ref, h_ref.at[tm:tm + ns])


def _ffn(xp, xs, g, w_in, w_out, *, tm, tf, side=()):
    ns = xs.shape[0]
    nf = D_FF // tf
    hf = tf // 2
    nh = D_FF // hf
    once = dict(pipeline_mode=pl.Buffered(1))
    side_rows = [a.shape[0] // SIDE_BLOCKS for a in side]
    side_block = lambda j: (jnp.minimum(j, SIDE_BLOCKS - 1), 0)
    side_specs = [pl.BlockSpec((r, a.shape[1]), side_block) for r, a in zip(side_rows, side)]
    head, wg, wu, wo, *side_bf16 = pl.pallas_call(
        functools.partial(_ffn_head_kernel, n_side=len(side)),
        out_shape=(jax.ShapeDtypeStruct(xp.shape, F32),
                   jax.ShapeDtypeStruct((D_MODEL, D_FF), BF16),
                   jax.ShapeDtypeStruct((D_MODEL, D_FF), BF16),
                   jax.ShapeDtypeStruct((D_FF, D_MODEL), BF16),
                   *[jax.ShapeDtypeStruct(a.shape, BF16) for a in side]),
        grid=(nh,),
        in_specs=[
            pl.BlockSpec((tm, D_MODEL), lambda j: (0, 0), **once),
            pl.BlockSpec((1, D_MODEL), lambda j: (0, 0), **once),
            pl.BlockSpec((D_MODEL, hf), lambda j: (0, j)),
            pl.BlockSpec((D_MODEL, hf), lambda j: (0, j + nh)),
            pl.BlockSpec((hf, D_MODEL), lambda j: (j, 0)),
            *side_specs,
        ],
        out_specs=(pl.BlockSpec((tm, D_MODEL), lambda j: (0, 0)),
                   pl.BlockSpec((D_MODEL, hf), lambda j: (0, j)),
                   pl.BlockSpec((D_MODEL, hf), lambda j: (0, j)),
                   pl.BlockSpec((hf, D_MODEL), lambda j: (j, 0)),
                   *side_specs),
        scratch_shapes=[pltpu.VMEM((tm, D_MODEL), BF16)],
        compiler_params=_params(("arbitrary",)),
        name="ffn_head",
    )(xp, g, w_in, w_in, w_out, *side)
    out_p, out_s = pl.pallas_call(
        functools.partial(_ffn_body_kernel, tm=tm),
        out_shape=(jax.ShapeDtypeStruct(xp.shape, F32), jax.ShapeDtypeStruct(xs.shape, F32)),
        grid=(xp.shape[0] // tm - 1, nf),
        in_specs=[
            pl.BlockSpec(memory_space=pl.ANY),
            pl.BlockSpec((tm, D_MODEL), lambda i, j: (i + 1, 0)),
            pl.BlockSpec((ns, D_MODEL), lambda i, j: (0, 0), **once),
            pl.BlockSpec((1, D_MODEL), lambda i, j: (0, 0), **once),
            pl.BlockSpec((D_MODEL, tf), lambda i, j: (0, j)),
            pl.BlockSpec((D_MODEL, tf), lambda i, j: (0, j)),
            pl.BlockSpec((tf, D_MODEL), lambda i, j: (j, 0)),
        ],
        out_specs=(pl.BlockSpec((tm, D_MODEL), lambda i, j: (i + 1, 0)),
                   pl.BlockSpec((ns, D_MODEL), lambda i, j: (0, 0))),
        scratch_shapes=[pltpu.VMEM((tm + ns, D_MODEL), BF16)],
        input_output_aliases={0: 0},
        compiler_params=_params(("arbitrary", "arbitrary")),
        name="ffn_body",
    )(head, xp, xs, g, wg, wu, wo)
    return out_p, out_s, side_bf16


def _mix_in_kernel(x_ref, g_ref, w_ref, v_ref, u_ref):
    h = _rms(x_ref[...], g_ref[...]).astype(BF16)
    nc = 512
    for c in range(CONV_DIM // nc):
        a_val = jnp.dot(h, w_ref[:, c * nc:(c + 1) * nc], preferred_element_type=F32)
        a_gate = jnp.dot(h, w_ref[:, CONV_DIM + c * nc:CONV_DIM + (c + 1) * nc],
                         preferred_element_type=F32)
        v_ref[:, c * nc:(c + 1) * nc] = a_val * jax.nn.sigmoid(a_gate)
        u_ref[:, c * nc:(c + 1) * nc] = jnp.dot(
            h, w_ref[:, 2 * CONV_DIM + c * nc:2 * CONV_DIM + (c + 1) * nc],
            preferred_element_type=F32)


def _mix_in(x, g, w_in, *, tm):
    rows = x.shape[0]
    return pl.pallas_call(
        _mix_in_kernel,
        out_shape=(jax.ShapeDtypeStruct((rows, CONV_DIM), F32),
                   jax.ShapeDtypeStruct((rows, POOL_DIM), F32)),
        grid=(rows // tm,),
        in_specs=[
            pl.BlockSpec((tm, D_MODEL), lambda i: (i, 0)),
            _resident((1, D_MODEL)),
            _resident(w_in.shape),
        ],
        out_specs=(pl.BlockSpec((tm, CONV_DIM), lambda i: (i, 0)),
                   pl.BlockSpec((tm, POOL_DIM), lambda i: (i, 0))),
        compiler_params=_params(("parallel",)),
        name="mix_in",
    )(x, g, w_in)


def _conv_post(acc, cb, lg, lb):
    y = acc + cb
    mu = jnp.mean(y, axis=-1, keepdims=True)
    yc = y - mu
    var = jnp.mean(yc * yc, axis=-1, keepdims=True)
    z = yc * lax.rsqrt(var + EPS) * lg + lb
    return z * jax.nn.sigmoid(z)


def _mix_tail(a_ref, d_ref, x_ref, pw_ref, ps_ref, wo_ref, o_ref):
    for g in range(len(POOL_WINDOWS)):
        sl = slice(g * POOL_GROUP_DIM, (g + 1) * POOL_GROUP_DIM)
        y = jnp.dot(d_ref[:, sl], pw_ref[g], preferred_element_type=F32) * ps_ref[:, sl]
        a_ref[:, CONV_DIM + g * POOL_GROUP_DIM:CONV_DIM + (g + 1) * POOL_GROUP_DIM] = y.astype(BF16)
    o_ref[...] = x_ref[...] + jnp.dot(a_ref[...], wo_ref[...], preferred_element_type=F32)


def _rows_above(tiles, b):
    rot = [pltpu.roll(t, SUBLANES - b, axis=0) for t in tiles]
    own = lax.broadcasted_iota(jnp.int32, tiles[0].shape, 0) < SUBLANES - b
    return [jnp.where(own, rot[j], rot[j + 1]) for j in range(len(tiles) - 1)]


def _rows_below(tiles, d):
    rot = [pltpu.roll(t, d, axis=0) for t in tiles]
    own = lax.broadcasted_iota(jnp.int32, tiles[0].shape, 0) >= d
    return [jnp.where(own, rot[j], rot[max(j - 1, 0)]) for j in range(len(tiles))]


def _conv_chunk(v_ref, wb_ref, r0, sl):
    n_out = ROW_CHUNK // SUBLANES
    n_src = n_out + CONV_HALO // SUBLANES
    src = [v_ref[r0 + SUBLANES * j:r0 + SUBLANES * (j + 1), sl] for j in range(n_src)]
    acc = [None] * n_out
    for b in range(SUBLANES):
        taps = [k for k in range(CONV_WIDTH) if (CONV_HALO - CONV_CTX + k) % SUBLANES == b]
        if not taps:
            continue
        shifted = src if b == 0 else _rows_above(src, b)
        for k in taps:
            a = (CONV_HALO - CONV_CTX + k) // SUBLANES
            wk = wb_ref[k, :, sl]
            for i in range(n_out):
                term = wk * shifted[a + i]
                acc[i] = term if acc[i] is None else acc[i] + term
    return jnp.concatenate(acc, axis=0)


def _pool_chunk(u_ref, r0, pos0, d_ref):
    n_out = ROW_CHUNK // SUBLANES
    n_ctx = POOL_HALO // SUBLANES
    for g, w in enumerate(POOL_WINDOWS):
        sl = slice(g * POOL_GROUP_DIM, (g + 1) * POOL_GROUP_DIM)
        cur = [u_ref[r0 + SUBLANES * j:r0 + SUBLANES * (j + 1), sl] for j in range(n_out + n_ctx)]
        s, span = cur, 1
        while span < w:
            if span < SUBLANES:
                below = _rows_below(s, span)
            else:
                below = [s[0]] + s[:-1]
            s = [x + y for x, y in zip(s, below)]
            span *= 2
        d = []
        for i in range(n_out):
            pos = pos0 + r0 + SUBLANES * i + lax.broadcasted_iota(jnp.int32, cur[0].shape, 0)
            cnt = jnp.minimum(pos + 1, w).astype(F32)
            d.append(s[n_ctx + i] / cnt - cur[n_ctx + i])
        d_ref[r0:r0 + ROW_CHUNK, sl] = jnp.concatenate(d, axis=0).astype(BF16)


def _mix_prompt_kernel(x_ref, g_ref, w_ref, cw_ref, cb_ref, lg_ref, lb_ref, pw_ref, ps_ref,
                       a_ref, ctail_ref, ptail_ref, vext_ref, uext_ref, d_ref, wb_ref, *, tm, tiles_per_seq):
    s = pl.program_id(0)
    n_tiles = pl.num_programs(0) - 1
    slot = lax.rem(s, 2)
    v_new, u_new = vext_ref.at[slot], uext_ref.at[slot]
    v_old, u_old = vext_ref.at[1 - slot], uext_ref.at[1 - slot]

    @pl.when(s == 0)
    def _():
        vext_ref[1] = jnp.zeros(vext_ref.shape[1:], F32)
        uext_ref[1] = jnp.zeros(uext_ref.shape[1:], F32)
        for k in range(CONV_WIDTH):
            wb_ref[k] = jnp.broadcast_to(cw_ref[k:k + 1, :], wb_ref.shape[1:])

    h = _rms(x_ref[...], g_ref[...]).astype(BF16)
    nc = 512
    for c in range(CONV_DIM // nc):
        sl = slice(c * nc, (c + 1) * nc)
        a_val = jnp.dot(h, w_ref[:, sl], preferred_element_type=F32)
        a_gate = jnp.dot(h, w_ref[:, CONV_DIM + c * nc:CONV_DIM + (c + 1) * nc], preferred_element_type=F32)
        v_new[CONV_HALO:, sl] = a_val * jax.nn.sigmoid(a_gate)
        u_new[POOL_HALO:, sl] = jnp.dot(h, w_ref[:, 2 * CONV_DIM + c * nc:2 * CONV_DIM + (c + 1) * nc],
                                       preferred_element_type=F32)
    starts_seq = lax.rem(jnp.minimum(s, n_tiles - 1), tiles_per_seq) == 0
    v_new[0:CONV_HALO, :] = jnp.where(starts_seq, 0.0, v_old[tm:tm + CONV_HALO, :])
    u_new[0:POOL_HALO, :] = jnp.where(starts_seq, 0.0, u_old[tm:tm + POOL_HALO, :])

    pos0 = lax.rem(jnp.maximum(s - 1, 0), tiles_per_seq) * tm
    cb, lg, lb = cb_ref[...], lg_ref[...], lb_ref[...]
    for r in range(tm // ROW_CHUNK):
        r0 = r * ROW_CHUNK
        conv = jnp.concatenate(
            [_conv_chunk(v_old, wb_ref, r0, slice(c * LANE_GROUP, (c + 1) * LANE_GROUP))
             for c in range(CONV_DIM // LANE_GROUP)], axis=1)
        a_ref[r0:r0 + ROW_CHUNK, 0:CONV_DIM] = _conv_post(conv, cb, lg, lb).astype(BF16)
        _pool_chunk(u_old, r0, pos0, d_ref)
    for g in range(len(POOL_WINDOWS)):
        sl = slice(g * POOL_GROUP_DIM, (g + 1) * POOL_GROUP_DIM)
        y = jnp.dot(d_ref[:, sl], pw_ref[g], preferred_element_type=F32) * ps_ref[:, sl]
        a_ref[:, CONV_DIM + g * POOL_GROUP_DIM:CONV_DIM + (g + 1) * POOL_GROUP_DIM] = y.astype(BF16)
    ctail_ref[0] = v_old[tm:tm + CONV_HALO, :]
    ptail_ref[0] = u_old[tm:tm + POOL_HALO, :]


def _mix_prompt(x, g, w_in, cw, cb, lg, lb, pw, ps, *, batch, seq, tm):
    tiles_per_seq = seq // tm
    n_tiles = batch * tiles_per_seq
    cur = lambda s: (jnp.minimum(s, n_tiles - 1), 0)
    prev = lambda s: (jnp.maximum(s - 1, 0), 0)
    prev_seq = lambda s: (jnp.maximum(s - 1, 0) // tiles_per_seq, 0, 0)
    return pl.pallas_call(
        functools.partial(_mix_prompt_kernel, tm=tm, tiles_per_seq=tiles_per_seq),
        out_shape=(jax.ShapeDtypeStruct((batch * seq, D_MODEL), BF16),
                   jax.ShapeDtypeStruct((batch, CONV_HALO, CONV_DIM), F32),
                   jax.ShapeDtypeStruct((batch, POOL_HALO, POOL_DIM), F32)),
        grid=(n_tiles + 1,),
        in_specs=[
            pl.BlockSpec((tm, D_MODEL), cur),
            _resident(g.shape), _resident(w_in.shape), _resident(cw.shape), _resident(cb.shape),
            _resident(lg.shape), _resident(lb.shape), _resident(pw.shape), _resident(ps.shape),
        ],
        out_specs=(pl.BlockSpec((tm, D_MODEL), prev),
                   pl.BlockSpec((1, CONV_HALO, CONV_DIM), prev_seq),
                   pl.BlockSpec((1, POOL_HALO, POOL_DIM), prev_seq)),
        scratch_shapes=[
            pltpu.VMEM((2, CONV_HALO + tm, CONV_DIM), F32),
            pltpu.VMEM((2, POOL_HALO + tm, POOL_DIM), F32),
            pltpu.VMEM((tm, POOL_DIM), BF16),
            pltpu.VMEM((CONV_WIDTH, SUBLANES, CONV_DIM), F32),
        ],
        compiler_params=_params(("arbitrary",)),
        name="mix_prompt",
    )(x, g, w_in, cw, cb, lg, lb, pw, ps)


def _out_proj_kernel(x_ref, a_ref, wo_ref, o_ref):
    o_ref[...] = x_ref[...] + jnp.dot(a_ref[...], wo_ref[...], preferred_element_type=F32)


def _out_proj(x, a, wo, *, tm):
    rows = x.shape[0]
    return pl.pallas_call(
        _out_proj_kernel,
        out_shape=jax.ShapeDtypeStruct((rows, D_MODEL), F32),
        grid=(rows // tm,),
        in_specs=[pl.BlockSpec((tm, D_MODEL), lambda i: (i, 0)),
                  pl.BlockSpec((tm, D_MODEL), lambda i: (i, 0)),
                  _resident(wo.shape)],
        out_specs=pl.BlockSpec((tm, D_MODEL), lambda i: (i, 0)),
        compiler_params=_params(("parallel",)),
        name="out_proj",
    )(x, a, wo)


def _mix_out_sample_kernel(v_ref, u_ref, sc_ref, sc_next_ref, sp_ref, sp_next_ref, x_ref, cw_ref, cb_ref, lg_ref, lb_ref,
                           pw_ref, ps_ref, wo_ref, o_ref, nsc_ref, nsp_ref, acc_ref, s_ref, a_ref, d_ref):
    c = pl.program_id(0)
    last = c == pl.num_programs(0) - 1

    @pl.when(c == 0)
    def _():
        acc_ref[...] = jnp.zeros(acc_ref.shape, F32)
        s_ref[...] = jnp.zeros(s_ref.shape, F32)

    acc = acc_ref[...]
    for r in range(CONV_STEP_ROWS):
        acc = acc + cw_ref[pl.ds(c * CONV_STEP_ROWS + r, 1), :] * sc_ref[r]
    acc_ref[...] = acc
    p = sp_ref[0]
    for g, w in enumerate(POOL_WINDOWS):
        sl = slice(g * POOL_GROUP_DIM, (g + 1) * POOL_GROUP_DIM)
        s_ref[:, sl] += jnp.where(c >= POOL_CTX + 1 - w, p[:, sl], 0.0)

    for r in range(CONV_STEP_ROWS - 1):
        nsc_ref[r] = sc_ref[r + 1]
    nsc_ref[CONV_STEP_ROWS - 1] = jnp.where(last, v_ref[...], sc_next_ref[0])
    nsp_ref[0] = jnp.where(last, u_ref[...], sp_next_ref[0])

    @pl.when(last)
    def _():
        conv = acc_ref[...] + cw_ref[CONV_CTX:CONV_WIDTH, :] * v_ref[...]
        a_ref[:, 0:CONV_DIM] = _conv_post(conv, cb_ref[...], lg_ref[...], lb_ref[...]).astype(BF16)
        u = u_ref[...]
        s = s_ref[...] + u
        lane1 = lax.broadcasted_iota(jnp.int32, (1, POOL_DIM), 1)
        cnt = jnp.zeros((1, POOL_DIM), F32)
        for g, w in enumerate(POOL_WINDOWS):
            cnt = jnp.where(lane1 // POOL_GROUP_DIM == g, float(min(PAST_LEN + 1, w)), cnt)
        d_ref[...] = (s / cnt - u).astype(BF16)
        _mix_tail(a_ref, d_ref, x_ref, pw_ref, ps_ref, wo_ref, o_ref)


def _mix_out_sample(v, u, sc, sp, x, cw, cb, lg, lb, pw, ps, wo):
    nb = v.shape[0]
    whole = (v, u, x, cw, cb, lg, lb, pw, ps, wo)
    v_s, u_s, x_s, *w_s = [_resident(a.shape) for a in whole]
    rows = lambda n, index: pl.BlockSpec((n, nb, sc.shape[2]), lambda c: (index(c), 0, 0))
    conv_rows = rows(CONV_STEP_ROWS, lambda c: c)
    conv_next = rows(1, lambda c: jnp.minimum((c + 1) * CONV_STEP_ROWS, CONV_CTX - 1))
    pool_row = rows(1, lambda c: c)
    pool_next = rows(1, lambda c: jnp.minimum(c + 1, POOL_CTX - 1))
    return pl.pallas_call(
        _mix_out_sample_kernel,
        out_shape=(jax.ShapeDtypeStruct((nb, D_MODEL), F32),
                   jax.ShapeDtypeStruct(sc.shape, F32), jax.ShapeDtypeStruct(sp.shape, F32)),
        grid=(POOL_CTX,),
        in_specs=[v_s, u_s, conv_rows, conv_next, pool_row, pool_next, x_s, *w_s],
        out_specs=(pl.BlockSpec((nb, D_MODEL), lambda c: (0, 0)), conv_rows, pool_row),
        scratch_shapes=[
            pltpu.VMEM((nb, CONV_DIM), F32),
            pltpu.VMEM((nb, POOL_DIM), F32),
            pltpu.VMEM((nb, D_MODEL), BF16),
            pltpu.VMEM((nb, POOL_DIM), BF16),
        ],
        compiler_params=_params(("arbitrary",)),
        name="mix_out_sample",
    )(v, u, sc, sc, sp, sp, x, cw, cb, lg, lb, pw, ps, wo)


def _ple_kernel(x_ref, p_ref, gp_ref, wg_ref, wp_ref, gf_ref, o_ref):
    x = x_ref[...]
    r = _rms(x, gp_ref[...]).astype(BF16)
    p = p_ref[...].astype(BF16)
    nc = 512
    for c in range(D_MODEL // nc):
        sl = slice(c * nc, (c + 1) * nc)
        gate = jax.nn.sigmoid(jnp.dot(r, wg_ref[:, sl], preferred_element_type=F32))
        proj = jnp.dot(p, wp_ref[:, sl], preferred_element_type=F32)
        o_ref[:, sl] = x_ref[:, sl] + gate * proj
    o_ref[...] = _rms(o_ref[...], gf_ref[...])


def _ple(x, p, gp, wg, wp, gf, *, tm):
    rows = x.shape[0]
    return pl.pallas_call(
        _ple_kernel,
        out_shape=jax.ShapeDtypeStruct((rows, D_MODEL), F32),
        grid=(rows // tm,),
        in_specs=[
            pl.BlockSpec((tm, D_MODEL), lambda i: (i, 0)),
            pl.BlockSpec((tm, PLE_DIM), lambda i: (i, 0)),
            _resident((1, D_MODEL)), _resident(wg.shape), _resident(wp.shape), _resident((1, D_MODEL)),
        ],
        out_specs=pl.BlockSpec((tm, D_MODEL), lambda i: (i, 0)),
        compiler_params=_params(("parallel",)),
        name="ple",
    )(x, p, gp, wg, wp, gf)


def kernel(x_prompt, x_sample, state_conv, state_pool, p_prompt, p_sample, norm_ffn1, w_ffn1_in, w_ffn1_out, norm_mix, w_in, conv_w, conv_b, conv_ln_g, conv_ln_b, pool_w, pool_scale, w_out, norm_ffn2, w_ffn2_in, w_ffn2_out, norm_ple, w_ple_gate, w_ple_proj, norm_final):
    assert norm_ffn1.shape[0] == 1, "the final norm is fused into the layer's last stage: one layer only"
    batch, seq, _ = x_prompt.shape
    nb = x_sample.shape[0]
    xp = x_prompt.reshape(batch * seq, D_MODEL)
    xs = x_sample.reshape(nb, D_MODEL)
    row = lambda a: a.reshape(1, -1)

    g1, gm, g2, gp, gf = row(norm_ffn1[0]), row(norm_mix[0]), row(norm_ffn2[0]), row(norm_ple[0]), row(norm_final)
    pool_w2d = pool_w[0].reshape(len(POOL_WINDOWS) * POOL_GROUP_DIM, POOL_GROUP_DIM)

    x1, x1s, (wi, wo, pw) = _ffn(xp, xs, g1, w_ffn1_in[0], w_ffn1_out[0], tm=FFN_TM, tf=FFN_TF,
                                 side=(w_in[0], w_out[0], pool_w2d))
    pw = pw.reshape(pool_w[0].shape)
    mix = (conv_w[0], row(conv_b[0]), row(conv_ln_g[0]), row(conv_ln_b[0]), pw, row(pool_scale[0]), wo)
    a, v_tail, u_tail = _mix_prompt(x1, gm, wi, *mix[:-1], batch=batch, seq=seq, tm=PROMPT_TM)
    x2 = _out_proj(x1, a, wo, tm=PROMPT_TM)
    vs, us = _mix_in(x1s, gm, wi, tm=nb)
    to_ctx_major = lambda st: jnp.transpose(st, (1, 0, 2))
    x2s, conv_t, pool_t = _mix_out_sample(vs, us, to_ctx_major(state_conv[0]), to_ctx_major(state_pool[0]), x1s, *mix)
    new_conv_s, new_pool_s = jnp.transpose(conv_t, (1, 0, 2)), jnp.transpose(pool_t, (1, 0, 2))
    x3, x3s, (wg, wp) = _ffn(x2, x2s, g2, w_ffn2_in[0], w_ffn2_out[0], tm=FFN_TM, tf=FFN_TF,
                             side=(w_ple_gate[0], w_ple_proj[0]))
    yp = _ple(x3, p_prompt[0].reshape(batch * seq, PLE_DIM), gp, wg, wp, gf, tm=PROMPT_TM)
    ys = _ple(x3s, p_sample[0].reshape(nb, PLE_DIM), gp, wg, wp, gf, tm=nb)
    new_conv_p = v_tail[:, CONV_HALO - CONV_CTX:]
    new_pool_p = u_tail[:, POOL_HALO - POOL_CTX:]

    return (yp.reshape(batch, seq, D_MODEL), ys.reshape(nb, 1, D_MODEL),
            new_conv_p[None], new_conv_s[None], new_pool_p[None], new_pool_s[None])
```

```python
import functools

import jax
import jax.numpy as jnp
from jax import lax
from jax.experimental import pallas as pl
from jax.experimental.pallas import tpu as pltpu

D_MODEL = 2048
D_FF = 5632
CONV_DIM = 1024
POOL_DIM = 1024
POOL_WINDOWS = (2, 4, 8, 16)
POOL_GROUP_DIM = POOL_DIM // len(POOL_WINDOWS)
POOL_CTX = max(POOL_WINDOWS) - 1
CONV_WIDTH = 31
CONV_CTX = CONV_WIDTH - 1
PLE_DIM = 256
PAST_LEN = 16384
EPS = 1e-6

F32 = jnp.float32
BF16 = jnp.bfloat16

VMEM_LIMIT_BYTES = 62 * 1024 * 1024
PROMPT_TM = 512
DENSE_TM = 1024
FFN_TM = 1024
FFN_TF = 512
CONV_HALO = 32
POOL_HALO = 16
ROW_CHUNK = 64
LANE_GROUP = 256
CONV_STEP_ROWS = CONV_CTX // POOL_CTX
SIDE_BLOCK_ROWS = 32
SUBLANES = 8


def _rms(x, g):
    ms = jnp.mean(x * x, axis=-1, keepdims=True)
    return x * lax.rsqrt(ms + EPS) * g


def _params(sem):
    return pltpu.CompilerParams(dimension_semantics=sem, vmem_limit_bytes=VMEM_LIMIT_BYTES)


def _resident(shape):
    nd = len(shape)
    return pl.BlockSpec(shape, lambda *_: (0,) * nd, pipeline_mode=pl.Buffered(1))


def _ffn_rows(j, x_ref, g_ref, w_refs, o_ref, h_ref, bf16_copy_refs=(None, None, None)):
    @pl.when(j == 0)
    def _():
        x = x_ref[...]
        h_ref[...] = _rms(x, g_ref[...]).astype(BF16)
        o_ref[...] = x

    def weight(k):
        w = w_refs[k][...].astype(BF16)
        if bf16_copy_refs[k] is not None:
            bf16_copy_refs[k][...] = w
        return w

    h = h_ref[...]
    gate = jnp.dot(h, weight(0), preferred_element_type=F32)
    up = jnp.dot(h, weight(1), preferred_element_type=F32)
    act = (gate * jax.nn.sigmoid(gate) * up * 0.5).astype(BF16)
    o_ref[...] += jnp.dot(act, weight(2), preferred_element_type=F32)


def _ffn_head_kernel(x_ref, g_ref, wg_ref, wu_ref, wo_ref, o_ref, wgb_ref, wub_ref, wob_ref, h_ref):
    _ffn_rows(pl.program_id(0), x_ref, g_ref, (wg_ref, wu_ref, wo_ref), o_ref, h_ref,
              bf16_copy_refs=(wgb_ref, wub_ref, wob_ref))


def _ffn_body_kernel(*refs, tm, n_side):
    head_ref, xp_ref, xs_ref, g_ref, wg_ref, wu_ref, wo_ref = refs[:7]
    side_in = refs[7:7 + n_side]
    op_ref, os_ref = refs[7 + n_side:9 + n_side]
    side_out = refs[9 + n_side:9 + 2 * n_side]
    h_ref = refs[9 + 2 * n_side]
    del head_ref
    i, j = pl.program_id(0), pl.program_id(1)
    ns = xs_ref.shape[0]
    for src, dst in zip(side_in, side_out):
        dst[...] = src[...].astype(BF16)
    w_refs = (wg_ref, wu_ref, wo_ref)
    _ffn_rows(j, xp_ref, g_ref, w_refs, op_ref, h_ref.at[0:tm])

    @pl.when(i == pl.num_programs(0) - 1)
    def _():
        _ffn_rows(j, xs_ref, g_ref, w_refs, os_ref, h_ref.at[tm:tm + ns])


def _ffn(xp, xs, g, w_in, w_out, *, tm, tf, side=()):
    ns = xs.shape[0]
    nf = D_FF // tf
    hf = tf // 2
    nh = D_FF // hf
    once = dict(pipeline_mode=pl.Buffered(1))
    head, wg, wu, wo = pl.pallas_call(
        _ffn_head_kernel,
        out_shape=(jax.ShapeDtypeStruct(xp.shape, F32),
                   jax.ShapeDtypeStruct((D_MODEL, D_FF), BF16),
                   jax.ShapeDtypeStruct((D_MODEL, D_FF), BF16),
                   jax.ShapeDtypeStruct((D_FF, D_MODEL), BF16)),
        grid=(nh,),
        in_specs=[
            pl.BlockSpec((tm, D_MODEL), lambda j: (0, 0), **once),
            pl.BlockSpec((1, D_MODEL), lambda j: (0, 0), **once),
            pl.BlockSpec((D_MODEL, hf), lambda j: (0, j)),
            pl.BlockSpec((D_MODEL, hf), lambda j: (0, j + nh)),
            pl.BlockSpec((hf, D_MODEL), lambda j: (j, 0)),
        ],
        out_specs=(pl.BlockSpec((tm, D_MODEL), lambda j: (0, 0)),
                   pl.BlockSpec((D_MODEL, hf), lambda j: (0, j)),
                   pl.BlockSpec((D_MODEL, hf), lambda j: (0, j)),
                   pl.BlockSpec((hf, D_MODEL), lambda j: (j, 0))),
        scratch_shapes=[pltpu.VMEM((tm, D_MODEL), BF16)],
        compiler_params=_params(("arbitrary",)),
        name="ffn_head",
    )(xp, g, w_in, w_in, w_out)

    n_body_tiles = xp.shape[0] // tm - 1
    side_rows = [max(SIDE_BLOCK_ROWS, -(-a.shape[0] // (n_body_tiles * nf))) for a in side]
    assert all(a.shape[0] % r == 0 for a, r in zip(side, side_rows))

    def side_spec(a, r):
        last = a.shape[0] // r - 1
        return pl.BlockSpec((r, a.shape[1]), lambda i, j: (jnp.minimum(i * nf + j, last), 0))

    side_specs = [side_spec(a, r) for a, r in zip(side, side_rows)]
    out_p, out_s, *side_bf16 = pl.pallas_call(
        functools.partial(_ffn_body_kernel, tm=tm, n_side=len(side)),
        out_shape=(jax.ShapeDtypeStruct(xp.shape, F32), jax.ShapeDtypeStruct(xs.shape, F32),
                   *[jax.ShapeDtypeStruct(a.shape, BF16) for a in side]),
        grid=(n_body_tiles, nf),
        in_specs=[
            pl.BlockSpec(memory_space=pl.ANY),
            pl.BlockSpec((tm, D_MODEL), lambda i, j: (i + 1, 0)),
            pl.BlockSpec((ns, D_MODEL), lambda i, j: (0, 0), **once),
            pl.BlockSpec((1, D_MODEL), lambda i, j: (0, 0), **once),
            pl.BlockSpec((D_MODEL, tf), lambda i, j: (0, j)),
            pl.BlockSpec((D_MODEL, tf), lambda i, j: (0, j)),
            pl.BlockSpec((tf, D_MODEL), lambda i, j: (j, 0)),
            *side_specs,
        ],
        out_specs=(pl.BlockSpec((tm, D_MODEL), lambda i, j: (i + 1, 0)),
                   pl.BlockSpec((ns, D_MODEL), lambda i, j: (0, 0)),
                   *side_specs),
        scratch_shapes=[pltpu.VMEM((tm + ns, D_MODEL), BF16)],
        input_output_aliases={0: 0},
        compiler_params=_params(("arbitrary", "arbitrary")),
        name="ffn_body",
    )(head, xp, xs, g, wg, wu, wo, *side)
    return out_p, out_s, side_bf16


def _mix_in_kernel(x_ref, g_ref, w_ref, v_ref, u_ref):
    h = _rms(x_ref[...], g_ref[...]).astype(BF16)
    nc = 512
    for c in range(CONV_DIM // nc):
        a_val = jnp.dot(h, w_ref[:, c * nc:(c + 1) * nc], preferred_element_type=F32)
        a_gate = jnp.dot(h, w_ref[:, CONV_DIM + c * nc:CONV_DIM + (c + 1) * nc],
                         preferred_element_type=F32)
        v_ref[:, c * nc:(c + 1) * nc] = a_val * jax.nn.sigmoid(a_gate)
        u_ref[:, c * nc:(c + 1) * nc] = jnp.dot(
            h, w_ref[:, 2 * CONV_DIM + c * nc:2 * CONV_DIM + (c + 1) * nc],
            preferred_element_type=F32)


def _mix_in(x, g, w_in, *, tm):
    rows = x.shape[0]
    return pl.pallas_call(
        _mix_in_kernel,
        out_shape=(jax.ShapeDtypeStruct((rows, CONV_DIM), F32),
                   jax.ShapeDtypeStruct((rows, POOL_DIM), F32)),
        grid=(rows // tm,),
        in_specs=[
            pl.BlockSpec((tm, D_MODEL), lambda i: (i, 0)),
            _resident((1, D_MODEL)),
            _resident(w_in.shape),
        ],
        out_specs=(pl.BlockSpec((tm, CONV_DIM), lambda i: (i, 0)),
                   pl.BlockSpec((tm, POOL_DIM), lambda i: (i, 0))),
        compiler_params=_params(("parallel",)),
        name="mix_in",
    )(x, g, w_in)


def _conv_post(acc, cb, lg, lb):
    y = acc + cb
    mu = jnp.mean(y, axis=-1, keepdims=True)
    yc = y - mu
    var = jnp.mean(yc * yc, axis=-1, keepdims=True)
    z = yc * lax.rsqrt(var + EPS) * lg + lb
    return z * jax.nn.sigmoid(z)


def _mix_tail(a_ref, d_ref, x_ref, pw_ref, ps_ref, wo_ref, o_ref):
    for g in range(len(POOL_WINDOWS)):
        sl = slice(g * POOL_GROUP_DIM, (g + 1) * POOL_GROUP_DIM)
        y = jnp.dot(d_ref[:, sl], pw_ref[g], preferred_element_type=F32) * ps_ref[:, sl]
        a_ref[:, CONV_DIM + g * POOL_GROUP_DIM:CONV_DIM + (g + 1) * POOL_GROUP_DIM] = y.astype(BF16)
    o_ref[...] = x_ref[...] + jnp.dot(a_ref[...], wo_ref[...], preferred_element_type=F32)


def _rows_above(tiles, b):
    rot = [pltpu.roll(t, SUBLANES - b, axis=0) for t in tiles]
    own = lax.broadcasted_iota(jnp.int32, tiles[0].shape, 0) < SUBLANES - b
    return [jnp.where(own, rot[j], rot[j + 1]) for j in range(len(tiles) - 1)]


def _rows_below(tiles, d):
    rot = [pltpu.roll(t, d, axis=0) for t in tiles]
    own = lax.broadcasted_iota(jnp.int32, tiles[0].shape, 0) >= d
    return [jnp.where(own, rot[j], rot[max(j - 1, 0)]) for j in range(len(tiles))]


def _conv_chunk(v_ref, wb_ref, r0, sl):
    n_out = ROW_CHUNK // SUBLANES
    n_src = n_out + CONV_HALO // SUBLANES
    src = [v_ref[r0 + SUBLANES * j:r0 + SUBLANES * (j + 1), sl] for j in range(n_src)]
    acc = [None] * n_out
    for b in range(SUBLANES):
        taps = [k for k in range(CONV_WIDTH) if (CONV_HALO - CONV_CTX + k) % SUBLANES == b]
        if not taps:
            continue
        shifted = src if b == 0 else _rows_above(src, b)
        for k in taps:
            a = (CONV_HALO - CONV_CTX + k) // SUBLANES
            wk = wb_ref[k, :, sl]
            for i in range(n_out):
                term = wk * shifted[a + i]
                acc[i] = term if acc[i] is None else acc[i] + term
    return jnp.concatenate(acc, axis=0)


def _pool_chunk(u_ref, r0, pos0, d_ref):
    n_out = ROW_CHUNK // SUBLANES
    n_ctx = POOL_HALO // SUBLANES
    for g, w in enumerate(POOL_WINDOWS):
        sl = slice(g * POOL_GROUP_DIM, (g + 1) * POOL_GROUP_DIM)
        cur = [u_ref[r0 + SUBLANES * j:r0 + SUBLANES * (j + 1), sl] for j in range(n_out + n_ctx)]
        s, span = cur, 1
        while span < w:
            if span < SUBLANES:
                below = _rows_below(s, span)
            else:
                below = [s[0]] + s[:-1]
            s = [x + y for x, y in zip(s, below)]
            span *= 2
        d = []
        for i in range(n_out):
            pos = pos0 + r0 + SUBLANES * i + lax.broadcasted_iota(jnp.int32, cur[0].shape, 0)
            cnt = jnp.minimum(pos + 1, w).astype(F32)
            d.append(s[n_ctx + i] / cnt - cur[n_ctx + i])
        d_ref[r0:r0 + ROW_CHUNK, sl] = jnp.concatenate(d, axis=0).astype(BF16)


def _mix_prompt_kernel(x_ref, g_ref, w_ref, cw_ref, cb_ref, lg_ref, lb_ref, pw_ref, ps_ref,
                       a_ref, ctail_ref, ptail_ref, vext_ref, uext_ref, d_ref, wb_ref, *, tm, tiles_per_seq):
    s = pl.program_id(0)
    n_tiles = pl.num_programs(0) - 1
    slot = lax.rem(s, 2)
    v_new, u_new = vext_ref.at[slot], uext_ref.at[slot]
    v_old, u_old = vext_ref.at[1 - slot], uext_ref.at[1 - slot]

    @pl.when(s == 0)
    def _():
        vext_ref[1] = jnp.zeros(vext_ref.shape[1:], F32)
        uext_ref[1] = jnp.zeros(uext_ref.shape[1:], F32)
        for k in range(CONV_WIDTH):
            wb_ref[k] = jnp.broadcast_to(cw_ref[k:k + 1, :], wb_ref.shape[1:])

    h = _rms(x_ref[...], g_ref[...]).astype(BF16)
    nc = 512
    for c in range(CONV_DIM // nc):
        sl = slice(c * nc, (c + 1) * nc)
        a_val = jnp.dot(h, w_ref[:, sl], preferred_element_type=F32)
        a_gate = jnp.dot(h, w_ref[:, CONV_DIM + c * nc:CONV_DIM + (c + 1) * nc], preferred_element_type=F32)
        v_new[CONV_HALO:, sl] = a_val * jax.nn.sigmoid(a_gate)
        u_new[POOL_HALO:, sl] = jnp.dot(h, w_ref[:, 2 * CONV_DIM + c * nc:2 * CONV_DIM + (c + 1) * nc],
                                       preferred_element_type=F32)
    starts_seq = lax.rem(jnp.minimum(s, n_tiles - 1), tiles_per_seq) == 0
    v_new[0:CONV_HALO, :] = jnp.where(starts_seq, 0.0, v_old[tm:tm + CONV_HALO, :])
    u_new[0:POOL_HALO, :] = jnp.where(starts_seq, 0.0, u_old[tm:tm + POOL_HALO, :])

    pos0 = lax.rem(jnp.maximum(s - 1, 0), tiles_per_seq) * tm
    cb, lg, lb = cb_ref[...], lg_ref[...], lb_ref[...]
    for r in range(tm // ROW_CHUNK):
        r0 = r * ROW_CHUNK
        conv = jnp.concatenate(
            [_conv_chunk(v_old, wb_ref, r0, slice(c * LANE_GROUP, (c + 1) * LANE_GROUP))
             for c in range(CONV_DIM // LANE_GROUP)], axis=1)
        a_ref[r0:r0 + ROW_CHUNK, 0:CONV_DIM] = _conv_post(conv, cb, lg, lb).astype(BF16)
        _pool_chunk(u_old, r0, pos0, d_ref)
    for g in range(len(POOL_WINDOWS)):
        sl = slice(g * POOL_GROUP_DIM, (g + 1) * POOL_GROUP_DIM)
        y = jnp.dot(d_ref[:, sl], pw_ref[g], preferred_element_type=F32) * ps_ref[:, sl]
        a_ref[:, CONV_DIM + g * POOL_GROUP_DIM:CONV_DIM + (g + 1) * POOL_GROUP_DIM] = y.astype(BF16)
    ctail_ref[0] = v_old[tm:tm + CONV_HALO, :]
    ptail_ref[0] = u_old[tm:tm + POOL_HALO, :]


def _mix_prompt(x, g, w_in, cw, cb, lg, lb, pw, ps, *, batch, seq, tm):
    tiles_per_seq = seq // tm
    n_tiles = batch * tiles_per_seq
    cur = lambda s: (jnp.minimum(s, n_tiles - 1), 0)
    prev = lambda s: (jnp.maximum(s - 1, 0), 0)
    prev_seq = lambda s: (jnp.maximum(s - 1, 0) // tiles_per_seq, 0, 0)
    return pl.pallas_call(
        functools.partial(_mix_prompt_kernel, tm=tm, tiles_per_seq=tiles_per_seq),
        out_shape=(jax.ShapeDtypeStruct((batch * seq, D_MODEL), BF16),
                   jax.ShapeDtypeStruct((batch, CONV_HALO, CONV_DIM), F32),
                   jax.ShapeDtypeStruct((batch, POOL_HALO, POOL_DIM), F32)),
        grid=(n_tiles + 1,),
        in_specs=[
            pl.BlockSpec((tm, D_MODEL), cur),
            _resident(g.shape), _resident(w_in.shape), _resident(cw.shape), _resident(cb.shape),
            _resident(lg.shape), _resident(lb.shape), _resident(pw.shape), _resident(ps.shape),
        ],
        out_specs=(pl.BlockSpec((tm, D_MODEL), prev),
                   pl.BlockSpec((1, CONV_HALO, CONV_DIM), prev_seq),
                   pl.BlockSpec((1, POOL_HALO, POOL_DIM), prev_seq)),
        scratch_shapes=[
            pltpu.VMEM((2, CONV_HALO + tm, CONV_DIM), F32),
            pltpu.VMEM((2, POOL_HALO + tm, POOL_DIM), F32),
            pltpu.VMEM((tm, POOL_DIM), BF16),
            pltpu.VMEM((CONV_WIDTH, SUBLANES, CONV_DIM), F32),
        ],
        compiler_params=_params(("arbitrary",)),
        name="mix_prompt",
    )(x, g, w_in, cw, cb, lg, lb, pw, ps)


def _out_proj_kernel(x_ref, a_ref, wo_ref, o_ref):
    o_ref[...] = x_ref[...] + jnp.dot(a_ref[...], wo_ref[...], preferred_element_type=F32)


def _out_proj(x, a, wo, *, tm):
    rows = x.shape[0]
    return pl.pallas_call(
        _out_proj_kernel,
        out_shape=jax.ShapeDtypeStruct((rows, D_MODEL), F32),
        grid=(rows // tm,),
        in_specs=[pl.BlockSpec((tm, D_MODEL), lambda i: (i, 0)),
                  pl.BlockSpec((tm, D_MODEL), lambda i: (i, 0)),
                  _resident(wo.shape)],
        out_specs=pl.BlockSpec((tm, D_MODEL), lambda i: (i, 0)),
        compiler_params=_params(("parallel",)),
        name="out_proj",
    )(x, a, wo)


def _mix_out_sample_kernel(v_ref, u_ref, sc_ref, sc_next_ref, sp_ref, sp_next_ref, x_ref, cw_ref, cb_ref, lg_ref, lb_ref,
                           pw_ref, ps_ref, wo_ref, o_ref, nsc_ref, nsp_ref, acc_ref, s_ref, a_ref, d_ref):
    c = pl.program_id(0)
    last = c == pl.num_programs(0) - 1

    @pl.when(c == 0)
    def _():
        acc_ref[...] = jnp.zeros(acc_ref.shape, F32)
        s_ref[...] = jnp.zeros(s_ref.shape, F32)

    acc = acc_ref[...]
    for r in range(CONV_STEP_ROWS):
        acc = acc + cw_ref[pl.ds(c * CONV_STEP_ROWS + r, 1), :] * sc_ref[r]
    acc_ref[...] = acc
    p = sp_ref[0]
    for g, w in enumerate(POOL_WINDOWS):
        sl = slice(g * POOL_GROUP_DIM, (g + 1) * POOL_GROUP_DIM)
        s_ref[:, sl] += jnp.where(c >= POOL_CTX + 1 - w, p[:, sl], 0.0)

    for r in range(CONV_STEP_ROWS - 1):
        nsc_ref[r] = sc_ref[r + 1]
    nsc_ref[CONV_STEP_ROWS - 1] = jnp.where(last, v_ref[...], sc_next_ref[0])
    nsp_ref[0] = jnp.where(last, u_ref[...], sp_next_ref[0])

    @pl.when(last)
    def _():
        conv = acc_ref[...] + cw_ref[CONV_CTX:CONV_WIDTH, :] * v_ref[...]
        a_ref[:, 0:CONV_DIM] = _conv_post(conv, cb_ref[...], lg_ref[...], lb_ref[...]).astype(BF16)
        u = u_ref[...]
        s = s_ref[...] + u
        lane1 = lax.broadcasted_iota(jnp.int32, (1, POOL_DIM), 1)
        cnt = jnp.zeros((1, POOL_DIM), F32)
        for g, w in enumerate(POOL_WINDOWS):
            cnt = jnp.where(lane1 // POOL_GROUP_DIM == g, float(min(PAST_LEN + 1, w)), cnt)
        d_ref[...] = (s / cnt - u).astype(BF16)
        _mix_tail(a_ref, d_ref, x_ref, pw_ref, ps_ref, wo_ref, o_ref)


def _mix_out_sample(v, u, sc, sp, x, cw, cb, lg, lb, pw, ps, wo):
    nb = v.shape[0]
    whole = (v, u, x, cw, cb, lg, lb, pw, ps, wo)
    v_s, u_s, x_s, *w_s = [_resident(a.shape) for a in whole]
    rows = lambda n, index: pl.BlockSpec((n, nb, sc.shape[2]), lambda c: (index(c), 0, 0))
    conv_rows = rows(CONV_STEP_ROWS, lambda c: c)
    conv_next = rows(1, lambda c: jnp.minimum((c + 1) * CONV_STEP_ROWS, CONV_CTX - 1))
    pool_row = rows(1, lambda c: c)
    pool_next = rows(1, lambda c: jnp.minimum(c + 1, POOL_CTX - 1))
    return pl.pallas_call(
        _mix_out_sample_kernel,
        out_shape=(jax.ShapeDtypeStruct((nb, D_MODEL), F32),
                   jax.ShapeDtypeStruct(sc.shape, F32), jax.ShapeDtypeStruct(sp.shape, F32)),
        grid=(POOL_CTX,),
        in_specs=[v_s, u_s, conv_rows, conv_next, pool_row, pool_next, x_s, *w_s],
        out_specs=(pl.BlockSpec((nb, D_MODEL), lambda c: (0, 0)), conv_rows, pool_row),
        scratch_shapes=[
            pltpu.VMEM((nb, CONV_DIM), F32),
            pltpu.VMEM((nb, POOL_DIM), F32),
            pltpu.VMEM((nb, D_MODEL), BF16),
            pltpu.VMEM((nb, POOL_DIM), BF16),
        ],
        compiler_params=_params(("arbitrary",)),
        name="mix_out_sample",
    )(v, u, sc, sc, sp, sp, x, cw, cb, lg, lb, pw, ps, wo)


def _ple_kernel(x_ref, p_ref, gp_ref, wg_ref, wp_ref, gf_ref, o_ref):
    x = x_ref[...]
    r = _rms(x, gp_ref[...]).astype(BF16)
    p = p_ref[...].astype(BF16)
    nc = 512
    for c in range(D_MODEL // nc):
        sl = slice(c * nc, (c + 1) * nc)
        gate = jax.nn.sigmoid(jnp.dot(r, wg_ref[:, sl], preferred_element_type=F32))
        proj = jnp.dot(p, wp_ref[:, sl], preferred_element_type=F32)
        o_ref[:, sl] = x_ref[:, sl] + gate * proj
    o_ref[...] = _rms(o_ref[...], gf_ref[...])


def _ple(x, p, gp, wg, wp, gf, *, tm):
    rows = x.shape[0]
    return pl.pallas_call(
        _ple_kernel,
        out_shape=jax.ShapeDtypeStruct((rows, D_MODEL), F32),
        grid=(rows // tm,),
        in_specs=[
            pl.BlockSpec((tm, D_MODEL), lambda i: (i, 0)),
            pl.BlockSpec((tm, PLE_DIM), lambda i: (i, 0)),
            _resident((1, D_MODEL)), _resident(wg.shape), _resident(wp.shape), _resident((1, D_MODEL)),
        ],
        out_specs=pl.BlockSpec((tm, D_MODEL), lambda i: (i, 0)),
        compiler_params=_params(("parallel",)),
        name="ple",
    )(x, p, gp, wg, wp, gf)


def kernel(x_prompt, x_sample, state_conv, state_pool, p_prompt, p_sample, norm_ffn1, w_ffn1_in, w_ffn1_out, norm_mix, w_in, conv_w, conv_b, conv_ln_g, conv_ln_b, pool_w, pool_scale, w_out, norm_ffn2, w_ffn2_in, w_ffn2_out, norm_ple, w_ple_gate, w_ple_proj, norm_final):
    assert norm_ffn1.shape[0] == 1, "the final norm is fused into the layer's last stage: one layer only"
    batch, seq, _ = x_prompt.shape
    nb = x_sample.shape[0]
    xp = x_prompt.reshape(batch * seq, D_MODEL)
    xs = x_sample.reshape(nb, D_MODEL)
    row = lambda a: a.reshape(1, -1)

    g1, gm, g2, gp, gf = row(norm_ffn1[0]), row(norm_mix[0]), row(norm_ffn2[0]), row(norm_ple[0]), row(norm_final)
    pool_w2d = pool_w[0].reshape(len(POOL_WINDOWS) * POOL_GROUP_DIM, POOL_GROUP_DIM)

    x1, x1s, (wi, wo, pw) = _ffn(xp, xs, g1, w_ffn1_in[0], w_ffn1_out[0], tm=FFN_TM, tf=FFN_TF,
                                 side=(w_in[0], w_out[0], pool_w2d))
    pw = pw.reshape(pool_w[0].shape)
    mix = (conv_w[0], row(conv_b[0]), row(conv_ln_g[0]), row(conv_ln_b[0]), pw, row(pool_scale[0]), wo)
    a, v_tail, u_tail = _mix_prompt(x1, gm, wi, *mix[:-1], batch=batch, seq=seq, tm=PROMPT_TM)
    x2 = _out_proj(x1, a, wo, tm=DENSE_TM)
    vs, us = _mix_in(x1s, gm, wi, tm=nb)
    to_ctx_major = lambda st: jnp.transpose(st, (1, 0, 2))
    x2s, conv_t, pool_t = _mix_out_sample(vs, us, to_ctx_major(state_conv[0]), to_ctx_major(state_pool[0]), x1s, *mix)
    new_conv_s, new_pool_s = jnp.transpose(conv_t, (1, 0, 2)), jnp.transpose(pool_t, (1, 0, 2))
    x3, x3s, (wg, wp) = _ffn(x2, x2s, g2, w_ffn2_in[0], w_ffn2_out[0], tm=FFN_TM, tf=FFN_TF,
                             side=(w_ple_gate[0], w_ple_proj[0]))
    yp = _ple(x3, p_prompt[0].reshape(batch * seq, PLE_DIM), gp, wg, wp, gf, tm=PROMPT_TM)
    ys = _ple(x3s, p_sample[0].reshape(nb, PLE_DIM), gp, wg, wp, gf, tm=nb)
    new_conv_p = v_tail[:, CONV_HALO - CONV_CTX:]
    new_pool_p = u_tail[:, POOL_HALO - POOL_CTX:]

    return (yp.reshape(batch, seq, D_MODEL), ys.reshape(nb, 1, D_MODEL),
            new_conv_p[None], new_conv_s[None], new_pool_p[None], new_pool_s[None])
```

```python
import functools
import math

import jax
import jax.numpy as jnp
from jax import lax
from jax.experimental import pallas as pl
from jax.experimental.pallas import tpu as pltpu

D_MODEL = 2048
D_FF = 5632
CONV_DIM = 1024
POOL_DIM = 1024
POOL_WINDOWS = (2, 4, 8, 16)
POOL_GROUP_DIM = POOL_DIM // len(POOL_WINDOWS)
POOL_CTX = max(POOL_WINDOWS) - 1
CONV_WIDTH = 31
CONV_CTX = CONV_WIDTH - 1
PLE_DIM = 256
PAST_LEN = 16384
EPS = 1e-6

F32 = jnp.float32
BF16 = jnp.bfloat16

V7X_VMEM_BYTES = 64 * 1024 * 1024
VMEM_HEADROOM_BYTES = 2 * 1024 * 1024
VMEM_TEMP_BYTES = 16 * 1024 * 1024
PROMPT_TM = 512
DENSE_TM = 1024
FFN_TM = 1024
FFN_TF = 512
CONV_HALO = 32
POOL_HALO = 16
ROW_CHUNK = 64
LANE_GROUP = 256
SAMPLE_STEPS = 5
CONV_STEP_ROWS = CONV_CTX // SAMPLE_STEPS
POOL_STEP_ROWS = POOL_CTX // SAMPLE_STEPS
SIDE_BLOCK_ROWS = 32
SUBLANES = 8


def _rms(x, g):
    ms = jnp.mean(x * x, axis=-1, keepdims=True)
    return x * lax.rsqrt(ms + EPS) * g


def _window_bytes(spec, dtype):
    if spec.block_shape is None:
        return 0
    buffers = 1 if spec.pipeline_mode is not None else 2
    return math.prod(spec.block_shape) * jnp.dtype(dtype).itemsize * buffers


def _call(kernel, *args, name, grid, in_specs, out_specs, out_shape, semantics, scratch_shapes=(), **kwargs):
    outs, ospecs = (out_shape, out_specs) if isinstance(out_shape, tuple) else ((out_shape,), (out_specs,))
    need = (sum(_window_bytes(sp, a.dtype) for sp, a in zip(in_specs, args))
            + sum(_window_bytes(sp, o.dtype) for sp, o in zip(ospecs, outs))
            + sum(math.prod(sc.shape) * jnp.dtype(sc.dtype).itemsize for sc in scratch_shapes))
    limit = min(need + VMEM_TEMP_BYTES, V7X_VMEM_BYTES - VMEM_HEADROOM_BYTES)
    call = pl.pallas_call(
        kernel, out_shape=out_shape, grid=grid, in_specs=in_specs, out_specs=out_specs,
        scratch_shapes=list(scratch_shapes), name=name,
        compiler_params=pltpu.CompilerParams(dimension_semantics=semantics, vmem_limit_bytes=limit), **kwargs)
    return call(*args)


def _resident(shape):
    nd = len(shape)
    return pl.BlockSpec(shape, lambda *_: (0,) * nd, pipeline_mode=pl.Buffered(1))


def _ffn_rows(j, x_ref, g_ref, w_refs, o_ref, h_ref, bf16_copy_refs=(None, None, None)):
    @pl.when(j == 0)
    def _():
        x = x_ref[...]
        h_ref[...] = _rms(x, g_ref[...]).astype(BF16)
        o_ref[...] = x

    def weight(k):
        w = w_refs[k][...].astype(BF16)
        if bf16_copy_refs[k] is not None:
            bf16_copy_refs[k][...] = w
        return w

    h = h_ref[...]
    gate = jnp.dot(h, weight(0), preferred_element_type=F32)
    up = jnp.dot(h, weight(1), preferred_element_type=F32)
    act = (gate * jax.nn.sigmoid(gate) * up * 0.5).astype(BF16)
    o_ref[...] += jnp.dot(act, weight(2), preferred_element_type=F32)


def _ffn_head_kernel(x_ref, g_ref, wg_ref, wu_ref, wo_ref, o_ref, wgb_ref, wub_ref, wob_ref, h_ref):
    _ffn_rows(pl.program_id(0), x_ref, g_ref, (wg_ref, wu_ref, wo_ref), o_ref, h_ref,
              bf16_copy_refs=(wgb_ref, wub_ref, wob_ref))


def _ffn_body_kernel(*refs, tm, n_side):
    head_ref, xp_ref, xs_ref, g_ref, wg_ref, wu_ref, wo_ref = refs[:7]
    side_in = refs[7:7 + n_side]
    op_ref, os_ref = refs[7 + n_side:9 + n_side]
    side_out = refs[9 + n_side:9 + 2 * n_side]
    h_ref = refs[9 + 2 * n_side]
    del head_ref
    i, j = pl.program_id(0), pl.program_id(1)
    ns = xs_ref.shape[0]
    for src, dst in zip(side_in, side_out):
        dst[...] = src[...].astype(BF16)
    w_refs = (wg_ref, wu_ref, wo_ref)
    _ffn_rows(j, xp_ref, g_ref, w_refs, op_ref, h_ref.at[0:tm])

    @pl.when(i == pl.num_programs(0) - 1)
    def _():
        _ffn_rows(j, xs_ref, g_ref, w_refs, os_ref, h_ref.at[tm:tm + ns])


def _ffn(xp, xs, g, w_in, w_out, *, tm, tf, side=()):
    ns = xs.shape[0]
    nf = D_FF // tf
    hf = tf // 2
    nh = D_FF // hf
    once = dict(pipeline_mode=pl.Buffered(1))
    head, wg, wu, wo = _call(
        _ffn_head_kernel, xp, g, w_in, w_in, w_out,
        name="ffn_head", semantics=("arbitrary",),
        out_shape=(jax.ShapeDtypeStruct(xp.shape, F32),
                   jax.ShapeDtypeStruct((D_MODEL, D_FF), BF16),
                   jax.ShapeDtypeStruct((D_MODEL, D_FF), BF16),
                   jax.ShapeDtypeStruct((D_FF, D_MODEL), BF16)),
        grid=(nh,),
        in_specs=[
            pl.BlockSpec((tm, D_MODEL), lambda j: (0, 0), **once),
            pl.BlockSpec((1, D_MODEL), lambda j: (0, 0), **once),
            pl.BlockSpec((D_MODEL, hf), lambda j: (0, j)),
            pl.BlockSpec((D_MODEL, hf), lambda j: (0, j + nh)),
            pl.BlockSpec((hf, D_MODEL), lambda j: (j, 0)),
        ],
        out_specs=(pl.BlockSpec((tm, D_MODEL), lambda j: (0, 0)),
                   pl.BlockSpec((D_MODEL, hf), lambda j: (0, j)),
                   pl.BlockSpec((D_MODEL, hf), lambda j: (0, j)),
                   pl.BlockSpec((hf, D_MODEL), lambda j: (j, 0))),
        scratch_shapes=[pltpu.VMEM((tm, D_MODEL), BF16)],
    )

    n_body_tiles = xp.shape[0] // tm - 1
    side_rows = [max(SIDE_BLOCK_ROWS, -(-a.shape[0] // (n_body_tiles * nf))) for a in side]
    assert all(a.shape[0] % r == 0 for a, r in zip(side, side_rows))

    def side_spec(a, r):
        last = a.shape[0] // r - 1
        return pl.BlockSpec((r, a.shape[1]), lambda i, j: (jnp.minimum(i * nf + j, last), 0))

    side_specs = [side_spec(a, r) for a, r in zip(side, side_rows)]
    out_p, out_s, *side_bf16 = _call(
        functools.partial(_ffn_body_kernel, tm=tm, n_side=len(side)), head, xp, xs, g, wg, wu, wo, *side,
        name="ffn_body", semantics=("arbitrary", "arbitrary"),
        out_shape=(jax.ShapeDtypeStruct(xp.shape, F32), jax.ShapeDtypeStruct(xs.shape, F32),
                   *[jax.ShapeDtypeStruct(a.shape, BF16) for a in side]),
        grid=(n_body_tiles, nf),
        in_specs=[
            pl.BlockSpec(memory_space=pl.ANY),
            pl.BlockSpec((tm, D_MODEL), lambda i, j: (i + 1, 0)),
            pl.BlockSpec((ns, D_MODEL), lambda i, j: (0, 0), **once),
            pl.BlockSpec((1, D_MODEL), lambda i, j: (0, 0), **once),
            pl.BlockSpec((D_MODEL, tf), lambda i, j: (0, j)),
            pl.BlockSpec((D_MODEL, tf), lambda i, j: (0, j)),
            pl.BlockSpec((tf, D_MODEL), lambda i, j: (j, 0)),
            *side_specs,
        ],
        out_specs=(pl.BlockSpec((tm, D_MODEL), lambda i, j: (i + 1, 0)),
                   pl.BlockSpec((ns, D_MODEL), lambda i, j: (0, 0)),
                   *side_specs),
        scratch_shapes=[pltpu.VMEM((tm + ns, D_MODEL), BF16)],
        input_output_aliases={0: 0},
    )
    return out_p, out_s, side_bf16


def _mix_in_kernel(x_ref, g_ref, w_ref, v_ref, u_ref):
    h = _rms(x_ref[...], g_ref[...]).astype(BF16)
    nc = 512
    for c in range(CONV_DIM // nc):
        a_val = jnp.dot(h, w_ref[:, c * nc:(c + 1) * nc], preferred_element_type=F32)
        a_gate = jnp.dot(h, w_ref[:, CONV_DIM + c * nc:CONV_DIM + (c + 1) * nc],
                         preferred_element_type=F32)
        v_ref[:, c * nc:(c + 1) * nc] = a_val * jax.nn.sigmoid(a_gate)
        u_ref[:, c * nc:(c + 1) * nc] = jnp.dot(
            h, w_ref[:, 2 * CONV_DIM + c * nc:2 * CONV_DIM + (c + 1) * nc],
            preferred_element_type=F32)


def _mix_in(x, g, w_in, *, tm):
    rows = x.shape[0]
    return _call(
        _mix_in_kernel, x, g, w_in,
        name="mix_in", semantics=("parallel",),
        out_shape=(jax.ShapeDtypeStruct((rows, CONV_DIM), F32),
                   jax.ShapeDtypeStruct((rows, POOL_DIM), F32)),
        grid=(rows // tm,),
        in_specs=[
            pl.BlockSpec((tm, D_MODEL), lambda i: (i, 0)),
            _resident((1, D_MODEL)),
            _resident(w_in.shape),
        ],
        out_specs=(pl.BlockSpec((tm, CONV_DIM), lambda i: (i, 0)),
                   pl.BlockSpec((tm, POOL_DIM), lambda i: (i, 0))),
    )


def _conv_post(acc, cb, lg, lb):
    y = acc + cb
    mu = jnp.mean(y, axis=-1, keepdims=True)
    yc = y - mu
    var = jnp.mean(yc * yc, axis=-1, keepdims=True)
    z = yc * lax.rsqrt(var + EPS) * lg + lb
    return z * jax.nn.sigmoid(z)


def _mix_tail(a_ref, d_ref, x_ref, pw_ref, ps_ref, wo_ref, o_ref):
    for g in range(len(POOL_WINDOWS)):
        sl = slice(g * POOL_GROUP_DIM, (g + 1) * POOL_GROUP_DIM)
        y = jnp.dot(d_ref[:, sl], pw_ref[g], preferred_element_type=F32) * ps_ref[:, sl]
        a_ref[:, CONV_DIM + g * POOL_GROUP_DIM:CONV_DIM + (g + 1) * POOL_GROUP_DIM] = y.astype(BF16)
    o_ref[...] = x_ref[...] + jnp.dot(a_ref[...], wo_ref[...], preferred_element_type=F32)


def _rows_above(tiles, b):
    rot = [pltpu.roll(t, SUBLANES - b, axis=0) for t in tiles]
    own = lax.broadcasted_iota(jnp.int32, tiles[0].shape, 0) < SUBLANES - b
    return [jnp.where(own, rot[j], rot[j + 1]) for j in range(len(tiles) - 1)]


def _rows_below(tiles, d):
    rot = [pltpu.roll(t, d, axis=0) for t in tiles]
    own = lax.broadcasted_iota(jnp.int32, tiles[0].shape, 0) >= d
    return [jnp.where(own, rot[j], rot[max(j - 1, 0)]) for j in range(len(tiles))]


def _conv_chunk(v_ref, wb_ref, r0, sl):
    n_out = ROW_CHUNK // SUBLANES
    n_src = n_out + CONV_HALO // SUBLANES
    src = [v_ref[r0 + SUBLANES * j:r0 + SUBLANES * (j + 1), sl] for j in range(n_src)]
    acc = [None] * n_out
    for b in range(SUBLANES):
        taps = [k for k in range(CONV_WIDTH) if (CONV_HALO - CONV_CTX + k) % SUBLANES == b]
        if not taps:
            continue
        shifted = src if b == 0 else _rows_above(src, b)
        for k in taps:
            a = (CONV_HALO - CONV_CTX + k) // SUBLANES
            wk = wb_ref[k, :, sl]
            for i in range(n_out):
                term = wk * shifted[a + i]
                acc[i] = term if acc[i] is None else acc[i] + term
    return jnp.concatenate(acc, axis=0)


def _pool_chunk(u_ref, r0, pos0, d_ref):
    n_out = ROW_CHUNK // SUBLANES
    n_ctx = POOL_HALO // SUBLANES
    for g, w in enumerate(POOL_WINDOWS):
        sl = slice(g * POOL_GROUP_DIM, (g + 1) * POOL_GROUP_DIM)
        cur = [u_ref[r0 + SUBLANES * j:r0 + SUBLANES * (j + 1), sl] for j in range(n_out + n_ctx)]
        s, span = cur, 1
        while span < w:
            if span < SUBLANES:
                below = _rows_below(s, span)
            else:
                below = [s[0]] + s[:-1]
            s = [x + y for x, y in zip(s, below)]
            span *= 2
        d = []
        for i in range(n_out):
            pos = pos0 + r0 + SUBLANES * i + lax.broadcasted_iota(jnp.int32, cur[0].shape, 0)
            cnt = jnp.minimum(pos + 1, w).astype(F32)
            d.append(s[n_ctx + i] / cnt - cur[n_ctx + i])
        d_ref[r0:r0 + ROW_CHUNK, sl] = jnp.concatenate(d, axis=0).astype(BF16)


def _mix_prompt_kernel(x_ref, g_ref, w_ref, cw_ref, cb_ref, lg_ref, lb_ref, pw_ref, ps_ref,
                       a_ref, ctail_ref, ptail_ref, vext_ref, uext_ref, d_ref, wb_ref, *, tm, tiles_per_seq):
    s = pl.program_id(0)
    n_tiles = pl.num_programs(0) - 1
    slot = lax.rem(s, 2)
    v_new, u_new = vext_ref.at[slot], uext_ref.at[slot]
    v_old, u_old = vext_ref.at[1 - slot], uext_ref.at[1 - slot]

    @pl.when(s == 0)
    def _():
        vext_ref[1] = jnp.zeros(vext_ref.shape[1:], F32)
        uext_ref[1] = jnp.zeros(uext_ref.shape[1:], F32)
        for k in range(CONV_WIDTH):
            wb_ref[k] = jnp.broadcast_to(cw_ref[k:k + 1, :], wb_ref.shape[1:])

    h = _rms(x_ref[...], g_ref[...]).astype(BF16)
    nc = 512
    for c in range(CONV_DIM // nc):
        sl = slice(c * nc, (c + 1) * nc)
        a_val = jnp.dot(h, w_ref[:, sl], preferred_element_type=F32)
        a_gate = jnp.dot(h, w_ref[:, CONV_DIM + c * nc:CONV_DIM + (c + 1) * nc], preferred_element_type=F32)
        v_new[CONV_HALO:, sl] = a_val * jax.nn.sigmoid(a_gate)
        u_new[POOL_HALO:, sl] = jnp.dot(h, w_ref[:, 2 * CONV_DIM + c * nc:2 * CONV_DIM + (c + 1) * nc],
                                       preferred_element_type=F32)
    starts_seq = lax.rem(jnp.minimum(s, n_tiles - 1), tiles_per_seq) == 0
    v_new[0:CONV_HALO, :] = jnp.where(starts_seq, 0.0, v_old[tm:tm + CONV_HALO, :])
    u_new[0:POOL_HALO, :] = jnp.where(starts_seq, 0.0, u_old[tm:tm + POOL_HALO, :])

    pos0 = lax.rem(jnp.maximum(s - 1, 0), tiles_per_seq) * tm
    cb, lg, lb = cb_ref[...], lg_ref[...], lb_ref[...]
    for r in range(tm // ROW_CHUNK):
        r0 = r * ROW_CHUNK
        conv = jnp.concatenate(
            [_conv_chunk(v_old, wb_ref, r0, slice(c * LANE_GROUP, (c + 1) * LANE_GROUP))
             for c in range(CONV_DIM // LANE_GROUP)], axis=1)
        a_ref[r0:r0 + ROW_CHUNK, 0:CONV_DIM] = _conv_post(conv, cb, lg, lb).astype(BF16)
        _pool_chunk(u_old, r0, pos0, d_ref)
    for g in range(len(POOL_WINDOWS)):
        sl = slice(g * POOL_GROUP_DIM, (g + 1) * POOL_GROUP_DIM)
        y = jnp.dot(d_ref[:, sl], pw_ref[g], preferred_element_type=F32) * ps_ref[:, sl]
        a_ref[:, CONV_DIM + g * POOL_GROUP_DIM:CONV_DIM + (g + 1) * POOL_GROUP_DIM] = y.astype(BF16)
    ctail_ref[0] = v_old[tm:tm + CONV_HALO, :]
    ptail_ref[0] = u_old[tm:tm + POOL_HALO, :]


def _mix_prompt(x, g, w_in, cw, cb, lg, lb, pw, ps, *, batch, seq, tm):
    tiles_per_seq = seq // tm
    n_tiles = batch * tiles_per_seq
    cur = lambda s: (jnp.minimum(s, n_tiles - 1), 0)
    prev = lambda s: (jnp.maximum(s - 1, 0), 0)
    prev_seq = lambda s: (jnp.maximum(s - 1, 0) // tiles_per_seq, 0, 0)
    return _call(
        functools.partial(_mix_prompt_kernel, tm=tm, tiles_per_seq=tiles_per_seq), x, g, w_in, cw, cb, lg, lb, pw, ps,
        name="mix_prompt", semantics=("arbitrary",),
        out_shape=(jax.ShapeDtypeStruct((batch * seq, D_MODEL), BF16),
                   jax.ShapeDtypeStruct((batch, CONV_HALO, CONV_DIM), F32),
                   jax.ShapeDtypeStruct((batch, POOL_HALO, POOL_DIM), F32)),
        grid=(n_tiles + 1,),
        in_specs=[
            pl.BlockSpec((tm, D_MODEL), cur),
            _resident(g.shape), _resident(w_in.shape), _resident(cw.shape), _resident(cb.shape),
            _resident(lg.shape), _resident(lb.shape), _resident(pw.shape), _resident(ps.shape),
        ],
        out_specs=(pl.BlockSpec((tm, D_MODEL), prev),
                   pl.BlockSpec((1, CONV_HALO, CONV_DIM), prev_seq),
                   pl.BlockSpec((1, POOL_HALO, POOL_DIM), prev_seq)),
        scratch_shapes=[
            pltpu.VMEM((2, CONV_HALO + tm, CONV_DIM), F32),
            pltpu.VMEM((2, POOL_HALO + tm, POOL_DIM), F32),
            pltpu.VMEM((tm, POOL_DIM), BF16),
            pltpu.VMEM((CONV_WIDTH, SUBLANES, CONV_DIM), F32),
        ],
    )


def _out_proj_kernel(x_ref, a_ref, wo_ref, o_ref):
    o_ref[...] = x_ref[...] + jnp.dot(a_ref[...], wo_ref[...], preferred_element_type=F32)


def _out_proj(x, a, wo, *, tm):
    rows = x.shape[0]
    return _call(
        _out_proj_kernel, x, a, wo,
        name="out_proj", semantics=("parallel",),
        out_shape=jax.ShapeDtypeStruct((rows, D_MODEL), F32),
        grid=(rows // tm,),
        in_specs=[pl.BlockSpec((tm, D_MODEL), lambda i: (i, 0)),
                  pl.BlockSpec((tm, D_MODEL), lambda i: (i, 0)),
                  _resident(wo.shape)],
        out_specs=pl.BlockSpec((tm, D_MODEL), lambda i: (i, 0)),
    )


def _mix_out_sample_kernel(v_ref, u_ref, sc_ref, sc_next_ref, sp_ref, sp_next_ref, x_ref, cw_ref, cb_ref, lg_ref, lb_ref,
                           pw_ref, ps_ref, wo_ref, o_ref, nsc_ref, nsp_ref, acc_ref, s_ref, a_ref, d_ref):
    c = pl.program_id(0)
    last = c == pl.num_programs(0) - 1

    @pl.when(c == 0)
    def _():
        acc_ref[...] = jnp.zeros(acc_ref.shape, F32)
        s_ref[...] = jnp.zeros(s_ref.shape, F32)

    acc = acc_ref[...]
    for r in range(CONV_STEP_ROWS):
        acc = acc + cw_ref[pl.ds(c * CONV_STEP_ROWS + r, 1), :] * sc_ref[r]
    acc_ref[...] = acc
    for g, w in enumerate(POOL_WINDOWS):
        sl = slice(g * POOL_GROUP_DIM, (g + 1) * POOL_GROUP_DIM)
        s = s_ref[:, sl]
        for r in range(POOL_STEP_ROWS):
            s = s + jnp.where(c * POOL_STEP_ROWS + r >= POOL_CTX + 1 - w, sp_ref[r, :, sl], 0.0)
        s_ref[:, sl] = s

    for r in range(CONV_STEP_ROWS - 1):
        nsc_ref[r] = sc_ref[r + 1]
    nsc_ref[CONV_STEP_ROWS - 1] = jnp.where(last, v_ref[...], sc_next_ref[0])
    for r in range(POOL_STEP_ROWS - 1):
        nsp_ref[r] = sp_ref[r + 1]
    nsp_ref[POOL_STEP_ROWS - 1] = jnp.where(last, u_ref[...], sp_next_ref[0])

    @pl.when(last)
    def _():
        conv = acc_ref[...] + cw_ref[CONV_CTX:CONV_WIDTH, :] * v_ref[...]
        a_ref[:, 0:CONV_DIM] = _conv_post(conv, cb_ref[...], lg_ref[...], lb_ref[...]).astype(BF16)
        u = u_ref[...]
        s = s_ref[...] + u
        lane1 = lax.broadcasted_iota(jnp.int32, (1, POOL_DIM), 1)
        cnt = jnp.zeros((1, POOL_DIM), F32)
        for g, w in enumerate(POOL_WINDOWS):
            cnt = jnp.where(lane1 // POOL_GROUP_DIM == g, float(min(PAST_LEN + 1, w)), cnt)
        d_ref[...] = (s / cnt - u).astype(BF16)
        _mix_tail(a_ref, d_ref, x_ref, pw_ref, ps_ref, wo_ref, o_ref)


def _mix_out_sample(v, u, sc, sp, x, cw, cb, lg, lb, pw, ps, wo):
    nb = v.shape[0]
    whole = (v, u, x, cw, cb, lg, lb, pw, ps, wo)
    v_s, u_s, x_s, *w_s = [_resident(a.shape) for a in whole]
    rows = lambda n, index: pl.BlockSpec((n, nb, sc.shape[2]), lambda c: (index(c), 0, 0))
    conv_rows = rows(CONV_STEP_ROWS, lambda c: c)
    conv_next = rows(1, lambda c: jnp.minimum((c + 1) * CONV_STEP_ROWS, CONV_CTX - 1))
    pool_rows = rows(POOL_STEP_ROWS, lambda c: c)
    pool_next = rows(1, lambda c: jnp.minimum((c + 1) * POOL_STEP_ROWS, POOL_CTX - 1))
    return _call(
        _mix_out_sample_kernel, v, u, sc, sc, sp, sp, x, cw, cb, lg, lb, pw, ps, wo,
        name="mix_out_sample", semantics=("arbitrary",),
        out_shape=(jax.ShapeDtypeStruct((nb, D_MODEL), F32),
                   jax.ShapeDtypeStruct(sc.shape, F32), jax.ShapeDtypeStruct(sp.shape, F32)),
        grid=(SAMPLE_STEPS,),
        in_specs=[v_s, u_s, conv_rows, conv_next, pool_rows, pool_next, x_s, *w_s],
        out_specs=(pl.BlockSpec((nb, D_MODEL), lambda c: (0, 0)), conv_rows, pool_rows),
        scratch_shapes=[
            pltpu.VMEM((nb, CONV_DIM), F32),
            pltpu.VMEM((nb, POOL_DIM), F32),
            pltpu.VMEM((nb, D_MODEL), BF16),
            pltpu.VMEM((nb, POOL_DIM), BF16),
        ],
    )


def _ple_kernel(x_ref, p_ref, gp_ref, wg_ref, wp_ref, gf_ref, o_ref):
    x = x_ref[...]
    r = _rms(x, gp_ref[...]).astype(BF16)
    p = p_ref[...].astype(BF16)
    nc = 512
    for c in range(D_MODEL // nc):
        sl = slice(c * nc, (c + 1) * nc)
        gate = jax.nn.sigmoid(jnp.dot(r, wg_ref[:, sl], preferred_element_type=F32))
        proj = jnp.dot(p, wp_ref[:, sl], preferred_element_type=F32)
        o_ref[:, sl] = x_ref[:, sl] + gate * proj
    o_ref[...] = _rms(o_ref[...], gf_ref[...])


def _ple(x, p, gp, wg, wp, gf, *, tm):
    rows = x.shape[0]
    return _call(
        _ple_kernel, x, p, gp, wg, wp, gf,
        name="ple", semantics=("parallel",),
        out_shape=jax.ShapeDtypeStruct((rows, D_MODEL), F32),
        grid=(rows // tm,),
        in_specs=[
            pl.BlockSpec((tm, D_MODEL), lambda i: (i, 0)),
            pl.BlockSpec((tm, PLE_DIM), lambda i: (i, 0)),
            _resident((1, D_MODEL)), _resident(wg.shape), _resident(wp.shape), _resident((1, D_MODEL)),
        ],
        out_specs=pl.BlockSpec((tm, D_MODEL), lambda i: (i, 0)),
    )


def kernel(x_prompt, x_sample, state_conv, state_pool, p_prompt, p_sample, norm_ffn1, w_ffn1_in, w_ffn1_out, norm_mix, w_in, conv_w, conv_b, conv_ln_g, conv_ln_b, pool_w, pool_scale, w_out, norm_ffn2, w_ffn2_in, w_ffn2_out, norm_ple, w_ple_gate, w_ple_proj, norm_final):
    assert norm_ffn1.shape[0] == 1, "the final norm is fused into the layer's last stage: one layer only"
    batch, seq, _ = x_prompt.shape
    nb = x_sample.shape[0]
    xp = x_prompt.reshape(batch * seq, D_MODEL)
    xs = x_sample.reshape(nb, D_MODEL)
    row = lambda a: a.reshape(1, -1)

    g1, gm, g2, gp, gf = row(norm_ffn1[0]), row(norm_mix[0]), row(norm_ffn2[0]), row(norm_ple[0]), row(norm_final)
    pool_w2d = pool_w[0].reshape(len(POOL_WINDOWS) * POOL_GROUP_DIM, POOL_GROUP_DIM)

    x1, x1s, (wi, wo, pw) = _ffn(xp, xs, g1, w_ffn1_in[0], w_ffn1_out[0], tm=FFN_TM, tf=FFN_TF,
                                 side=(w_in[0], w_out[0], pool_w2d))
    pw = pw.reshape(pool_w[0].shape)
    mix = (conv_w[0], row(conv_b[0]), row(conv_ln_g[0]), row(conv_ln_b[0]), pw, row(pool_scale[0]), wo)
    a, v_tail, u_tail = _mix_prompt(x1, gm, wi, *mix[:-1], batch=batch, seq=seq, tm=PROMPT_TM)
    x2 = _out_proj(x1, a, wo, tm=DENSE_TM)
    vs, us = _mix_in(x1s, gm, wi, tm=nb)
    to_ctx_major = lambda st: jnp.transpose(st, (1, 0, 2))
    x2s, conv_t, pool_t = _mix_out_sample(vs, us, to_ctx_major(state_conv[0]), to_ctx_major(state_pool[0]), x1s, *mix)
    new_conv_s, new_pool_s = jnp.transpose(conv_t, (1, 0, 2)), jnp.transpose(pool_t, (1, 0, 2))
    x3, x3s, (wg, wp) = _ffn(x2, x2s, g2, w_ffn2_in[0], w_ffn2_out[0], tm=FFN_TM, tf=FFN_TF,
                             side=(w_ple_gate[0], w_ple_proj[0]))
    yp = _ple(x3, p_prompt[0].reshape(batch * seq, PLE_DIM), gp, wg, wp, gf, tm=PROMPT_TM)
    ys = _ple(x3s, p_sample[0].reshape(nb, PLE_DIM), gp, wg, wp, gf, tm=nb)
    new_conv_p = v_tail[:, CONV_HALO - CONV_CTX:]
    new_pool_p = u_tail[:, POOL_HALO - POOL_CTX:]

    return (yp.reshape(batch, seq, D_MODEL), ys.reshape(nb, 1, D_MODEL),
            new_conv_p[None], new_conv_s[None], new_pool_p[None], new_pool_s[None])
```

```python
import functools
import math

import jax
import jax.numpy as jnp
from jax import lax
from jax.experimental import pallas as pl
from jax.experimental.pallas import tpu as pltpu

D_MODEL = 2048
D_FF = 5632
CONV_DIM = 1024
POOL_DIM = 1024
POOL_WINDOWS = (2, 4, 8, 16)
POOL_GROUP_DIM = POOL_DIM // len(POOL_WINDOWS)
POOL_CTX = max(POOL_WINDOWS) - 1
CONV_WIDTH = 31
CONV_CTX = CONV_WIDTH - 1
PLE_DIM = 256
PAST_LEN = 16384
EPS = 1e-6

F32 = jnp.float32
BF16 = jnp.bfloat16

V7X_VMEM_BYTES = 64 * 1024 * 1024
VMEM_HEADROOM_BYTES = 2 * 1024 * 1024
VMEM_TEMP_BYTES = 16 * 1024 * 1024
PROMPT_TM = 512
DENSE_TM = 1024
FFN_TM = 1024
FFN_TF = 512
CONV_HALO = 32
POOL_HALO = 16
ROW_CHUNK = 64
LANE_GROUP = 256
SAMPLE_STEPS = 5
CONV_STEP_ROWS = CONV_CTX // SAMPLE_STEPS
POOL_STEP_ROWS = POOL_CTX // SAMPLE_STEPS
SIDE_BLOCK_ROWS = 32
SUBLANES = 8


def _rms(x, g):
    ms = jnp.mean(x * x, axis=-1, keepdims=True)
    return x * lax.rsqrt(ms + EPS) * g


def _window_bytes(spec, dtype):
    if spec.block_shape is None:
        return 0
    buffers = 1 if spec.pipeline_mode is not None else 2
    return math.prod(spec.block_shape) * jnp.dtype(dtype).itemsize * buffers


def _call(kernel, *args, name, grid, in_specs, out_specs, out_shape, semantics, scratch_shapes=(), **kwargs):
    outs, ospecs = (out_shape, out_specs) if isinstance(out_shape, tuple) else ((out_shape,), (out_specs,))
    need = (sum(_window_bytes(sp, a.dtype) for sp, a in zip(in_specs, args))
            + sum(_window_bytes(sp, o.dtype) for sp, o in zip(ospecs, outs))
            + sum(math.prod(sc.shape) * jnp.dtype(sc.dtype).itemsize for sc in scratch_shapes))
    limit = min(need + VMEM_TEMP_BYTES, V7X_VMEM_BYTES - VMEM_HEADROOM_BYTES)
    call = pl.pallas_call(
        kernel, out_shape=out_shape, grid=grid, in_specs=in_specs, out_specs=out_specs,
        scratch_shapes=list(scratch_shapes), name=name,
        compiler_params=pltpu.CompilerParams(dimension_semantics=semantics, vmem_limit_bytes=limit), **kwargs)
    return call(*args)


def _resident(shape):
    nd = len(shape)
    return pl.BlockSpec(shape, lambda *_: (0,) * nd, pipeline_mode=pl.Buffered(1))


def _ffn_rows(j, x_ref, g_ref, w_refs, o_ref, h_ref, bf16_copy_refs=(None, None, None)):
    @pl.when(j == 0)
    def _():
        x = x_ref[...]
        h_ref[...] = _rms(x, g_ref[...]).astype(BF16)
        o_ref[...] = x

    def weight(k):
        w = w_refs[k][...].astype(BF16)
        if bf16_copy_refs[k] is not None:
            bf16_copy_refs[k][...] = w
        return w

    h = h_ref[...]
    gate = jnp.dot(h, weight(0), preferred_element_type=F32)
    up = jnp.dot(h, weight(1), preferred_element_type=F32)
    act = (gate * jax.nn.sigmoid(gate) * up * 0.5).astype(BF16)
    o_ref[...] += jnp.dot(act, weight(2), preferred_element_type=F32)


def _ffn_head_kernel(x_ref, g_ref, wg_ref, wu_ref, wo_ref, o_ref, wgb_ref, wub_ref, wob_ref, h_ref):
    _ffn_rows(pl.program_id(0), x_ref, g_ref, (wg_ref, wu_ref, wo_ref), o_ref, h_ref,
              bf16_copy_refs=(wgb_ref, wub_ref, wob_ref))


def _ffn_body_kernel(*refs, tm, n_side):
    head_ref, xp_ref, xs_ref, g_ref, wg_ref, wu_ref, wo_ref = refs[:7]
    side_in = refs[7:7 + n_side]
    op_ref, os_ref = refs[7 + n_side:9 + n_side]
    side_out = refs[9 + n_side:9 + 2 * n_side]
    h_ref = refs[9 + 2 * n_side]
    del head_ref
    i, j = pl.program_id(0), pl.program_id(1)
    ns = xs_ref.shape[0]
    for src, dst in zip(side_in, side_out):
        dst[...] = src[...].astype(BF16)
    w_refs = (wg_ref, wu_ref, wo_ref)
    _ffn_rows(j, xp_ref, g_ref, w_refs, op_ref, h_ref.at[0:tm])

    @pl.when(i == pl.num_programs(0) - 1)
    def _():
        _ffn_rows(j, xs_ref, g_ref, w_refs, os_ref, h_ref.at[tm:tm + ns])


def _ffn(xp, xs, g, w_in, w_out, *, tm, tf, side=()):
    ns = xs.shape[0]
    nf = D_FF // tf
    hf = tf // 2
    nh = D_FF // hf
    once = dict(pipeline_mode=pl.Buffered(1))
    head, wg, wu, wo = _call(
        _ffn_head_kernel, xp, g, w_in, w_in, w_out,
        name="ffn_head", semantics=("arbitrary",),
        out_shape=(jax.ShapeDtypeStruct(xp.shape, F32),
                   jax.ShapeDtypeStruct((D_MODEL, D_FF), BF16),
                   jax.ShapeDtypeStruct((D_MODEL, D_FF), BF16),
                   jax.ShapeDtypeStruct((D_FF, D_MODEL), BF16)),
        grid=(nh,),
        in_specs=[
            pl.BlockSpec((tm, D_MODEL), lambda j: (0, 0), **once),
            pl.BlockSpec((1, D_MODEL), lambda j: (0, 0), **once),
            pl.BlockSpec((D_MODEL, hf), lambda j: (0, j)),
            pl.BlockSpec((D_MODEL, hf), lambda j: (0, j + nh)),
            pl.BlockSpec((hf, D_MODEL), lambda j: (j, 0)),
        ],
        out_specs=(pl.BlockSpec((tm, D_MODEL), lambda j: (0, 0)),
                   pl.BlockSpec((D_MODEL, hf), lambda j: (0, j)),
                   pl.BlockSpec((D_MODEL, hf), lambda j: (0, j)),
                   pl.BlockSpec((hf, D_MODEL), lambda j: (j, 0))),
        scratch_shapes=[pltpu.VMEM((tm, D_MODEL), BF16)],
    )

    n_body_tiles = xp.shape[0] // tm - 1
    side_rows = [max(SIDE_BLOCK_ROWS, -(-a.shape[0] // (n_body_tiles * nf))) for a in side]
    assert all(a.shape[0] % r == 0 for a, r in zip(side, side_rows))

    def side_spec(a, r):
        last = a.shape[0] // r - 1
        return pl.BlockSpec((r, a.shape[1]), lambda i, j: (jnp.minimum(i * nf + j, last), 0))

    side_specs = [side_spec(a, r) for a, r in zip(side, side_rows)]
    out_p, out_s, *side_bf16 = _call(
        functools.partial(_ffn_body_kernel, tm=tm, n_side=len(side)), head, xp, xs, g, wg, wu, wo, *side,
        name="ffn_body", semantics=("arbitrary", "arbitrary"),
        out_shape=(jax.ShapeDtypeStruct(xp.shape, F32), jax.ShapeDtypeStruct(xs.shape, F32),
                   *[jax.ShapeDtypeStruct(a.shape, BF16) for a in side]),
        grid=(n_body_tiles, nf),
        in_specs=[
            pl.BlockSpec(memory_space=pl.ANY),
            pl.BlockSpec((tm, D_MODEL), lambda i, j: (i + 1, 0)),
            pl.BlockSpec((ns, D_MODEL), lambda i, j: (0, 0), **once),
            pl.BlockSpec((1, D_MODEL), lambda i, j: (0, 0), **once),
            pl.BlockSpec((D_MODEL, tf), lambda i, j: (0, j)),
            pl.BlockSpec((D_MODEL, tf), lambda i, j: (0, j)),
            pl.BlockSpec((tf, D_MODEL), lambda i, j: (j, 0)),
            *side_specs,
        ],
        out_specs=(pl.BlockSpec((tm, D_MODEL), lambda i, j: (i + 1, 0)),
                   pl.BlockSpec((ns, D_MODEL), lambda i, j: (0, 0)),
                   *side_specs),
        scratch_shapes=[pltpu.VMEM((tm + ns, D_MODEL), BF16)],
        input_output_aliases={0: 0},
    )
    return out_p, out_s, side_bf16


def _project(x, g_ref, w_ref, put_v, put_u):
    h = _rms(x, g_ref[...]).astype(BF16)
    nc = 512
    for c in range(CONV_DIM // nc):
        sl = slice(c * nc, (c + 1) * nc)
        a_val = jnp.dot(h, w_ref[:, sl], preferred_element_type=F32)
        a_gate = jnp.dot(h, w_ref[:, CONV_DIM + c * nc:CONV_DIM + (c + 1) * nc], preferred_element_type=F32)
        put_v(sl, a_val * jax.nn.sigmoid(a_gate))
        put_u(sl, jnp.dot(h, w_ref[:, 2 * CONV_DIM + c * nc:2 * CONV_DIM + (c + 1) * nc],
                          preferred_element_type=F32))


def _conv_post(acc, cb, lg, lb):
    y = acc + cb
    mu = jnp.mean(y, axis=-1, keepdims=True)
    yc = y - mu
    var = jnp.mean(yc * yc, axis=-1, keepdims=True)
    z = yc * lax.rsqrt(var + EPS) * lg + lb
    return z * jax.nn.sigmoid(z)


def _mix_tail(a_ref, d_ref, x_ref, pw_ref, ps_ref, wo_ref, o_ref):
    for g in range(len(POOL_WINDOWS)):
        sl = slice(g * POOL_GROUP_DIM, (g + 1) * POOL_GROUP_DIM)
        y = jnp.dot(d_ref[:, sl], pw_ref[g], preferred_element_type=F32) * ps_ref[:, sl]
        a_ref[:, CONV_DIM + g * POOL_GROUP_DIM:CONV_DIM + (g + 1) * POOL_GROUP_DIM] = y.astype(BF16)
    o_ref[...] = x_ref[...] + jnp.dot(a_ref[...], wo_ref[...], preferred_element_type=F32)


def _rows_above(tiles, b):
    rot = [pltpu.roll(t, SUBLANES - b, axis=0) for t in tiles]
    own = lax.broadcasted_iota(jnp.int32, tiles[0].shape, 0) < SUBLANES - b
    return [jnp.where(own, rot[j], rot[j + 1]) for j in range(len(tiles) - 1)]


def _rows_below(tiles, d):
    rot = [pltpu.roll(t, d, axis=0) for t in tiles]
    own = lax.broadcasted_iota(jnp.int32, tiles[0].shape, 0) >= d
    return [jnp.where(own, rot[j], rot[max(j - 1, 0)]) for j in range(len(tiles))]


def _conv_chunk(v_ref, wb_ref, r0, sl):
    n_out = ROW_CHUNK // SUBLANES
    n_src = n_out + CONV_HALO // SUBLANES
    src = [v_ref[r0 + SUBLANES * j:r0 + SUBLANES * (j + 1), sl] for j in range(n_src)]
    acc = [None] * n_out
    for b in range(SUBLANES):
        taps = [k for k in range(CONV_WIDTH) if (CONV_HALO - CONV_CTX + k) % SUBLANES == b]
        if not taps:
            continue
        shifted = src if b == 0 else _rows_above(src, b)
        for k in taps:
            a = (CONV_HALO - CONV_CTX + k) // SUBLANES
            wk = wb_ref[k, :, sl]
            for i in range(n_out):
                term = wk * shifted[a + i]
                acc[i] = term if acc[i] is None else acc[i] + term
    return jnp.concatenate(acc, axis=0)


def _pool_chunk(u_ref, r0, pos0, d_ref):
    n_out = ROW_CHUNK // SUBLANES
    n_ctx = POOL_HALO // SUBLANES
    for g, w in enumerate(POOL_WINDOWS):
        sl = slice(g * POOL_GROUP_DIM, (g + 1) * POOL_GROUP_DIM)
        cur = [u_ref[r0 + SUBLANES * j:r0 + SUBLANES * (j + 1), sl] for j in range(n_out + n_ctx)]
        s, span = cur, 1
        while span < w:
            if span < SUBLANES:
                below = _rows_below(s, span)
            else:
                below = [s[0]] + s[:-1]
            s = [x + y for x, y in zip(s, below)]
            span *= 2
        d = []
        for i in range(n_out):
            pos = pos0 + r0 + SUBLANES * i + lax.broadcasted_iota(jnp.int32, cur[0].shape, 0)
            cnt = jnp.minimum(pos + 1, w).astype(F32)
            d.append(s[n_ctx + i] / cnt - cur[n_ctx + i])
        d_ref[r0:r0 + ROW_CHUNK, sl] = jnp.concatenate(d, axis=0).astype(BF16)


def _mix_prompt_kernel(x_ref, xs_ref, g_ref, w_ref, cw_ref, cb_ref, lg_ref, lb_ref, pw_ref, ps_ref,
                       a_ref, ctail_ref, ptail_ref, vs_ref, us_ref, vext_ref, uext_ref, d_ref, wb_ref,
                       *, tm, tiles_per_seq):
    s = pl.program_id(0)
    n_tiles = pl.num_programs(0) - 1
    slot = lax.rem(s, 2)
    v_new, u_new = vext_ref.at[slot], uext_ref.at[slot]
    v_old, u_old = vext_ref.at[1 - slot], uext_ref.at[1 - slot]

    @pl.when(s == 0)
    def _():
        vext_ref[1] = jnp.zeros(vext_ref.shape[1:], F32)
        uext_ref[1] = jnp.zeros(uext_ref.shape[1:], F32)
        for k in range(CONV_WIDTH):
            wb_ref[k] = jnp.broadcast_to(cw_ref[k:k + 1, :], wb_ref.shape[1:])

    def put_v(sl, val):
        v_new[CONV_HALO:, sl] = val

    def put_u(sl, val):
        u_new[POOL_HALO:, sl] = val

    _project(x_ref[...], g_ref, w_ref, put_v, put_u)
    starts_seq = lax.rem(jnp.minimum(s, n_tiles - 1), tiles_per_seq) == 0
    v_new[0:CONV_HALO, :] = jnp.where(starts_seq, 0.0, v_old[tm:tm + CONV_HALO, :])
    u_new[0:POOL_HALO, :] = jnp.where(starts_seq, 0.0, u_old[tm:tm + POOL_HALO, :])

    pos0 = lax.rem(jnp.maximum(s - 1, 0), tiles_per_seq) * tm
    cb, lg, lb = cb_ref[...], lg_ref[...], lb_ref[...]
    for r in range(tm // ROW_CHUNK):
        r0 = r * ROW_CHUNK
        conv = jnp.concatenate(
            [_conv_chunk(v_old, wb_ref, r0, slice(c * LANE_GROUP, (c + 1) * LANE_GROUP))
             for c in range(CONV_DIM // LANE_GROUP)], axis=1)
        a_ref[r0:r0 + ROW_CHUNK, 0:CONV_DIM] = _conv_post(conv, cb, lg, lb).astype(BF16)
        _pool_chunk(u_old, r0, pos0, d_ref)
    for g in range(len(POOL_WINDOWS)):
        sl = slice(g * POOL_GROUP_DIM, (g + 1) * POOL_GROUP_DIM)
        y = jnp.dot(d_ref[:, sl], pw_ref[g], preferred_element_type=F32) * ps_ref[:, sl]
        a_ref[:, CONV_DIM + g * POOL_GROUP_DIM:CONV_DIM + (g + 1) * POOL_GROUP_DIM] = y.astype(BF16)
    ctail_ref[0] = v_old[tm:tm + CONV_HALO, :]
    ptail_ref[0] = u_old[tm:tm + POOL_HALO, :]

    @pl.when(s == n_tiles)
    def _():
        def put_vs(sl, val):
            vs_ref[:, sl] = val

        def put_us(sl, val):
            us_ref[:, sl] = val

        _project(xs_ref[...], g_ref, w_ref, put_vs, put_us)


def _mix_prompt(x, xs, g, w_in, cw, cb, lg, lb, pw, ps, *, batch, seq, tm):
    tiles_per_seq = seq // tm
    n_tiles = batch * tiles_per_seq
    ns = xs.shape[0]
    cur = lambda s: (jnp.minimum(s, n_tiles - 1), 0)
    prev = lambda s: (jnp.maximum(s - 1, 0), 0)
    prev_seq = lambda s: (jnp.maximum(s - 1, 0) // tiles_per_seq, 0, 0)
    return _call(
        functools.partial(_mix_prompt_kernel, tm=tm, tiles_per_seq=tiles_per_seq),
        x, xs, g, w_in, cw, cb, lg, lb, pw, ps,
        name="mix_prompt", semantics=("arbitrary",),
        out_shape=(jax.ShapeDtypeStruct((batch * seq, D_MODEL), BF16),
                   jax.ShapeDtypeStruct((batch, CONV_HALO, CONV_DIM), F32),
                   jax.ShapeDtypeStruct((batch, POOL_HALO, POOL_DIM), F32),
                   jax.ShapeDtypeStruct((ns, CONV_DIM), F32),
                   jax.ShapeDtypeStruct((ns, POOL_DIM), F32)),
        grid=(n_tiles + 1,),
        in_specs=[
            pl.BlockSpec((tm, D_MODEL), cur), _resident(xs.shape),
            _resident(g.shape), _resident(w_in.shape), _resident(cw.shape), _resident(cb.shape),
            _resident(lg.shape), _resident(lb.shape), _resident(pw.shape), _resident(ps.shape),
        ],
        out_specs=(pl.BlockSpec((tm, D_MODEL), prev),
                   pl.BlockSpec((1, CONV_HALO, CONV_DIM), prev_seq),
                   pl.BlockSpec((1, POOL_HALO, POOL_DIM), prev_seq),
                   pl.BlockSpec((ns, CONV_DIM), lambda s: (0, 0)),
                   pl.BlockSpec((ns, POOL_DIM), lambda s: (0, 0))),
        scratch_shapes=[
            pltpu.VMEM((2, CONV_HALO + tm, CONV_DIM), F32),
            pltpu.VMEM((2, POOL_HALO + tm, POOL_DIM), F32),
            pltpu.VMEM((tm, POOL_DIM), BF16),
            pltpu.VMEM((CONV_WIDTH, SUBLANES, CONV_DIM), F32),
        ],
    )


def _out_proj_kernel(x_ref, a_ref, wo_ref, o_ref):
    o_ref[...] = x_ref[...] + jnp.dot(a_ref[...], wo_ref[...], preferred_element_type=F32)


def _out_proj(x, a, wo, *, tm):
    rows = x.shape[0]
    return _call(
        _out_proj_kernel, x, a, wo,
        name="out_proj", semantics=("parallel",),
        out_shape=jax.ShapeDtypeStruct((rows, D_MODEL), F32),
        grid=(rows // tm,),
        in_specs=[pl.BlockSpec((tm, D_MODEL), lambda i: (i, 0)),
                  pl.BlockSpec((tm, D_MODEL), lambda i: (i, 0)),
                  _resident(wo.shape)],
        out_specs=pl.BlockSpec((tm, D_MODEL), lambda i: (i, 0)),
    )


def _mix_out_sample_kernel(v_ref, u_ref, sc_ref, sc_next_ref, sp_ref, sp_next_ref, x_ref, cw_ref, cb_ref, lg_ref, lb_ref,
                           pw_ref, ps_ref, wo_ref, o_ref, nsc_ref, nsp_ref, acc_ref, s_ref, a_ref, d_ref):
    c = pl.program_id(0)
    last = c == pl.num_programs(0) - 1

    @pl.when(c == 0)
    def _():
        acc_ref[...] = jnp.zeros(acc_ref.shape, F32)
        s_ref[...] = jnp.zeros(s_ref.shape, F32)

    acc = acc_ref[...]
    for r in range(CONV_STEP_ROWS):
        acc = acc + cw_ref[pl.ds(c * CONV_STEP_ROWS + r, 1), :] * sc_ref[r]
    acc_ref[...] = acc
    for g, w in enumerate(POOL_WINDOWS):
        sl = slice(g * POOL_GROUP_DIM, (g + 1) * POOL_GROUP_DIM)
        s = s_ref[:, sl]
        for r in range(POOL_STEP_ROWS):
            s = s + jnp.where(c * POOL_STEP_ROWS + r >= POOL_CTX + 1 - w, sp_ref[r, :, sl], 0.0)
        s_ref[:, sl] = s

    for r in range(CONV_STEP_ROWS - 1):
        nsc_ref[r] = sc_ref[r + 1]
    nsc_ref[CONV_STEP_ROWS - 1] = jnp.where(last, v_ref[...], sc_next_ref[0])
    for r in range(POOL_STEP_ROWS - 1):
        nsp_ref[r] = sp_ref[r + 1]
    nsp_ref[POOL_STEP_ROWS - 1] = jnp.where(last, u_ref[...], sp_next_ref[0])

    @pl.when(last)
    def _():
        conv = acc_ref[...] + cw_ref[CONV_CTX:CONV_WIDTH, :] * v_ref[...]
        a_ref[:, 0:CONV_DIM] = _conv_post(conv, cb_ref[...], lg_ref[...], lb_ref[...]).astype(BF16)
        u = u_ref[...]
        s = s_ref[...] + u
        lane1 = lax.broadcasted_iota(jnp.int32, (1, POOL_DIM), 1)
        cnt = jnp.zeros((1, POOL_DIM), F32)
        for g, w in enumerate(POOL_WINDOWS):
            cnt = jnp.where(lane1 // POOL_GROUP_DIM == g, float(min(PAST_LEN + 1, w)), cnt)
        d_ref[...] = (s / cnt - u).astype(BF16)
        _mix_tail(a_ref, d_ref, x_ref, pw_ref, ps_ref, wo_ref, o_ref)


def _mix_out_sample(v, u, sc, sp, x, cw, cb, lg, lb, pw, ps, wo):
    nb = v.shape[0]
    whole = (v, u, x, cw, cb, lg, lb, pw, ps, wo)
    v_s, u_s, x_s, *w_s = [_resident(a.shape) for a in whole]
    rows = lambda n, index: pl.BlockSpec((n, nb, sc.shape[2]), lambda c: (index(c), 0, 0))
    conv_rows = rows(CONV_STEP_ROWS, lambda c: c)
    conv_next = rows(1, lambda c: jnp.minimum((c + 1) * CONV_STEP_ROWS, CONV_CTX - 1))
    pool_rows = rows(POOL_STEP_ROWS, lambda c: c)
    pool_next = rows(1, lambda c: jnp.minimum((c + 1) * POOL_STEP_ROWS, POOL_CTX - 1))
    return _call(
        _mix_out_sample_kernel, v, u, sc, sc, sp, sp, x, cw, cb, lg, lb, pw, ps, wo,
        name="mix_out_sample", semantics=("arbitrary",),
        out_shape=(jax.ShapeDtypeStruct((nb, D_MODEL), F32),
                   jax.ShapeDtypeStruct(sc.shape, F32), jax.ShapeDtypeStruct(sp.shape, F32)),
        grid=(SAMPLE_STEPS,),
        in_specs=[v_s, u_s, conv_rows, conv_next, pool_rows, pool_next, x_s, *w_s],
        out_specs=(pl.BlockSpec((nb, D_MODEL), lambda c: (0, 0)), conv_rows, pool_rows),
        scratch_shapes=[
            pltpu.VMEM((nb, CONV_DIM), F32),
            pltpu.VMEM((nb, POOL_DIM), F32),
            pltpu.VMEM((nb, D_MODEL), BF16),
            pltpu.VMEM((nb, POOL_DIM), BF16),
        ],
    )


def _ple_rows(x_ref, p_ref, gp_ref, wg_ref, wp_ref, gf_ref, o_ref):
    r = _rms(x_ref[...], gp_ref[...]).astype(BF16)
    p = p_ref[...].astype(BF16)
    nc = 512
    for c in range(D_MODEL // nc):
        sl = slice(c * nc, (c + 1) * nc)
        gate = jax.nn.sigmoid(jnp.dot(r, wg_ref[:, sl], preferred_element_type=F32))
        proj = jnp.dot(p, wp_ref[:, sl], preferred_element_type=F32)
        o_ref[:, sl] = x_ref[:, sl] + gate * proj
    o_ref[...] = _rms(o_ref[...], gf_ref[...])


def _ple_kernel(x_ref, p_ref, xs_ref, psm_ref, gp_ref, wg_ref, wp_ref, gf_ref, o_ref, os_ref):
    _ple_rows(x_ref, p_ref, gp_ref, wg_ref, wp_ref, gf_ref, o_ref)

    @pl.when(pl.program_id(0) == pl.num_programs(0) - 1)
    def _():
        _ple_rows(xs_ref, psm_ref, gp_ref, wg_ref, wp_ref, gf_ref, os_ref)


def _ple(x, p, xs, ps, gp, wg, wp, gf, *, tm):
    rows, ns = x.shape[0], xs.shape[0]
    return _call(
        _ple_kernel, x, p, xs, ps, gp, wg, wp, gf,
        name="ple", semantics=("arbitrary",),
        out_shape=(jax.ShapeDtypeStruct((rows, D_MODEL), F32), jax.ShapeDtypeStruct((ns, D_MODEL), F32)),
        grid=(rows // tm,),
        in_specs=[
            pl.BlockSpec((tm, D_MODEL), lambda i: (i, 0)),
            pl.BlockSpec((tm, PLE_DIM), lambda i: (i, 0)),
            _resident(xs.shape), _resident(ps.shape),
            _resident((1, D_MODEL)), _resident(wg.shape), _resident(wp.shape), _resident((1, D_MODEL)),
        ],
        out_specs=(pl.BlockSpec((tm, D_MODEL), lambda i: (i, 0)),
                   pl.BlockSpec((ns, D_MODEL), lambda i: (0, 0))),
    )


def kernel(x_prompt, x_sample, state_conv, state_pool, p_prompt, p_sample, norm_ffn1, w_ffn1_in, w_ffn1_out, norm_mix, w_in, conv_w, conv_b, conv_ln_g, conv_ln_b, pool_w, pool_scale, w_out, norm_ffn2, w_ffn2_in, w_ffn2_out, norm_ple, w_ple_gate, w_ple_proj, norm_final):
    assert norm_ffn1.shape[0] == 1, "the final norm is fused into the layer's last stage: one layer only"
    batch, seq, _ = x_prompt.shape
    nb = x_sample.shape[0]
    xp = x_prompt.reshape(batch * seq, D_MODEL)
    xs = x_sample.reshape(nb, D_MODEL)
    row = lambda a: a.reshape(1, -1)

    g1, gm, g2, gp, gf = row(norm_ffn1[0]), row(norm_mix[0]), row(norm_ffn2[0]), row(norm_ple[0]), row(norm_final)
    pool_w2d = pool_w[0].reshape(len(POOL_WINDOWS) * POOL_GROUP_DIM, POOL_GROUP_DIM)

    x1, x1s, (wi, wo, pw) = _ffn(xp, xs, g1, w_ffn1_in[0], w_ffn1_out[0], tm=FFN_TM, tf=FFN_TF,
                                 side=(w_in[0], w_out[0], pool_w2d))
    pw = pw.reshape(pool_w[0].shape)
    mix = (conv_w[0], row(conv_b[0]), row(conv_ln_g[0]), row(conv_ln_b[0]), pw, row(pool_scale[0]), wo)
    a, v_tail, u_tail, vs, us = _mix_prompt(x1, x1s, gm, wi, *mix[:-1], batch=batch, seq=seq, tm=PROMPT_TM)
    x2 = _out_proj(x1, a, wo, tm=DENSE_TM)
    to_ctx_major = lambda st: jnp.transpose(st, (1, 0, 2))
    x2s, conv_t, pool_t = _mix_out_sample(vs, us, to_ctx_major(state_conv[0]), to_ctx_major(state_pool[0]), x1s, *mix)
    new_conv_s, new_pool_s = jnp.transpose(conv_t, (1, 0, 2)), jnp.transpose(pool_t, (1, 0, 2))
    x3, x3s, (wg, wp) = _ffn(x2, x2s, g2, w_ffn2_in[0], w_ffn2_out[0], tm=FFN_TM, tf=FFN_TF,
                             side=(w_ple_gate[0], w_ple_proj[0]))
    yp, ys = _ple(x3, p_prompt[0].reshape(batch * seq, PLE_DIM), x3s, p_sample[0].reshape(nb, PLE_DIM),
                  gp, wg, wp, gf, tm=PROMPT_TM)
    new_conv_p = v_tail[:, CONV_HALO - CONV_CTX:]
    new_pool_p = u_tail[:, POOL_HALO - POOL_CTX:]

    return (yp.reshape(batch, seq, D_MODEL), ys.reshape(nb, 1, D_MODEL),
            new_conv_p[None], new_conv_s[None], new_pool_p[None], new_pool_s[None])
```

```python
import functools
import math

import jax
import jax.numpy as jnp
from jax import lax
from jax.experimental import pallas as pl
from jax.experimental.pallas import tpu as pltpu

D_MODEL = 2048
D_FF = 5632
CONV_DIM = 1024
POOL_DIM = 1024
POOL_WINDOWS = (2, 4, 8, 16)
POOL_GROUP_DIM = POOL_DIM // len(POOL_WINDOWS)
POOL_CTX = max(POOL_WINDOWS) - 1
CONV_WIDTH = 31
CONV_CTX = CONV_WIDTH - 1
PLE_DIM = 256
PAST_LEN = 16384
EPS = 1e-6

F32 = jnp.float32
BF16 = jnp.bfloat16

V7X_VMEM_BYTES = 64 * 1024 * 1024
VMEM_HEADROOM_BYTES = 2 * 1024 * 1024
VMEM_TEMP_BYTES = 16 * 1024 * 1024
PROMPT_TM = 512
FFN_TM = 1024
FFN_TF = 512
CONV_HALO = 32
POOL_HALO = 16
ROW_CHUNK = 64
LANE_GROUP = 256
SAMPLE_STEPS = 5
CONV_STEP_ROWS = CONV_CTX // SAMPLE_STEPS
POOL_STEP_ROWS = POOL_CTX // SAMPLE_STEPS
SIDE_BLOCK_ROWS = 32
SUBLANES = 8


def _rms(x, g):
    ms = jnp.mean(x * x, axis=-1, keepdims=True)
    return x * lax.rsqrt(ms + EPS) * g


def _window_bytes(spec, dtype):
    if spec.block_shape is None:
        return 0
    buffers = 1 if spec.pipeline_mode is not None else 2
    return math.prod(spec.block_shape) * jnp.dtype(dtype).itemsize * buffers


def _call(kernel, *args, name, grid, in_specs, out_specs, out_shape, semantics, scratch_shapes=(), **kwargs):
    outs, ospecs = (out_shape, out_specs) if isinstance(out_shape, tuple) else ((out_shape,), (out_specs,))
    need = (sum(_window_bytes(sp, a.dtype) for sp, a in zip(in_specs, args))
            + sum(_window_bytes(sp, o.dtype) for sp, o in zip(ospecs, outs))
            + sum(math.prod(sc.shape) * jnp.dtype(sc.dtype).itemsize for sc in scratch_shapes))
    limit = min(need + VMEM_TEMP_BYTES, V7X_VMEM_BYTES - VMEM_HEADROOM_BYTES)
    call = pl.pallas_call(
        kernel, out_shape=out_shape, grid=grid, in_specs=in_specs, out_specs=out_specs,
        scratch_shapes=list(scratch_shapes), name=name,
        compiler_params=pltpu.CompilerParams(dimension_semantics=semantics, vmem_limit_bytes=limit), **kwargs)
    return call(*args)


def _resident(shape):
    nd = len(shape)
    return pl.BlockSpec(shape, lambda *_: (0,) * nd, pipeline_mode=pl.Buffered(1))


def _ffn_rows(j, x_ref, g_ref, w_refs, o_ref, h_ref, bf16_copy_refs=(None, None, None)):
    @pl.when(j == 0)
    def _():
        x = x_ref[...]
        h_ref[...] = _rms(x, g_ref[...]).astype(BF16)
        o_ref[...] = x

    def weight(k):
        w = w_refs[k][...].astype(BF16)
        if bf16_copy_refs[k] is not None:
            bf16_copy_refs[k][...] = w
        return w

    h = h_ref[...]
    gate = jnp.dot(h, weight(0), preferred_element_type=F32)
    up = jnp.dot(h, weight(1), preferred_element_type=F32)
    act = (gate * jax.nn.sigmoid(gate) * up * 0.5).astype(BF16)
    o_ref[...] += jnp.dot(act, weight(2), preferred_element_type=F32)


def _ffn_head_kernel(x_ref, g_ref, wg_ref, wu_ref, wo_ref, o_ref, wgb_ref, wub_ref, wob_ref, h_ref):
    _ffn_rows(pl.program_id(0), x_ref, g_ref, (wg_ref, wu_ref, wo_ref), o_ref, h_ref,
              bf16_copy_refs=(wgb_ref, wub_ref, wob_ref))


def _ffn_body_kernel(*refs, tm, n_side):
    xp_ref, xs_ref, g_ref, wg_ref, wu_ref, wo_ref = refs[:6]
    side_in = refs[6:6 + n_side]
    op_ref, os_ref = refs[6 + n_side:8 + n_side]
    side_out = refs[8 + n_side:8 + 2 * n_side]
    h_ref = refs[8 + 2 * n_side]
    i, j = pl.program_id(0), pl.program_id(1)
    ns = xs_ref.shape[0]
    for src, dst in zip(side_in, side_out):
        dst[...] = src[...].astype(BF16)
    w_refs = (wg_ref, wu_ref, wo_ref)
    _ffn_rows(j, xp_ref, g_ref, w_refs, op_ref, h_ref.at[0:tm])

    @pl.when(i == pl.num_programs(0) - 1)
    def _():
        _ffn_rows(j, xs_ref, g_ref, w_refs, os_ref, h_ref.at[tm:tm + ns])


def _ffn(xp, xs, g, w_in, w_out, *, tm, tf, side=()):
    ns = xs.shape[0]
    nf = D_FF // tf
    hf = tf // 2
    nh = D_FF // hf
    once = dict(pipeline_mode=pl.Buffered(1))
    head, wg, wu, wo = _call(
        _ffn_head_kernel, xp, g, w_in, w_in, w_out,
        name="ffn_head", semantics=("arbitrary",),
        out_shape=(jax.ShapeDtypeStruct((tm, D_MODEL), F32),
                   jax.ShapeDtypeStruct((D_MODEL, D_FF), BF16),
                   jax.ShapeDtypeStruct((D_MODEL, D_FF), BF16),
                   jax.ShapeDtypeStruct((D_FF, D_MODEL), BF16)),
        grid=(nh,),
        in_specs=[
            pl.BlockSpec((tm, D_MODEL), lambda j: (0, 0), **once),
            pl.BlockSpec((1, D_MODEL), lambda j: (0, 0), **once),
            pl.BlockSpec((D_MODEL, hf), lambda j: (0, j)),
            pl.BlockSpec((D_MODEL, hf), lambda j: (0, j + nh)),
            pl.BlockSpec((hf, D_MODEL), lambda j: (j, 0)),
        ],
        out_specs=(pl.BlockSpec((tm, D_MODEL), lambda j: (0, 0)),
                   pl.BlockSpec((D_MODEL, hf), lambda j: (0, j)),
                   pl.BlockSpec((D_MODEL, hf), lambda j: (0, j)),
                   pl.BlockSpec((hf, D_MODEL), lambda j: (j, 0))),
        scratch_shapes=[pltpu.VMEM((tm, D_MODEL), BF16)],
    )

    n_body_tiles = xp.shape[0] // tm - 1
    side_rows = [max(SIDE_BLOCK_ROWS, -(-a.shape[0] // (n_body_tiles * nf))) for a in side]
    assert all(a.shape[0] % r == 0 for a, r in zip(side, side_rows))

    def side_spec(a, r):
        last = a.shape[0] // r - 1
        return pl.BlockSpec((r, a.shape[1]), lambda i, j: (jnp.minimum(i * nf + j, last), 0))

    side_specs = [side_spec(a, r) for a, r in zip(side, side_rows)]
    rest, out_s, *side_bf16 = _call(
        functools.partial(_ffn_body_kernel, tm=tm, n_side=len(side)), xp, xs, g, wg, wu, wo, *side,
        name="ffn_body", semantics=("arbitrary", "arbitrary"),
        out_shape=(jax.ShapeDtypeStruct((xp.shape[0] - tm, D_MODEL), F32), jax.ShapeDtypeStruct(xs.shape, F32),
                   *[jax.ShapeDtypeStruct(a.shape, BF16) for a in side]),
        grid=(n_body_tiles, nf),
        in_specs=[
            pl.BlockSpec((tm, D_MODEL), lambda i, j: (i + 1, 0)),
            pl.BlockSpec((ns, D_MODEL), lambda i, j: (0, 0), **once),
            pl.BlockSpec((1, D_MODEL), lambda i, j: (0, 0), **once),
            pl.BlockSpec((D_MODEL, tf), lambda i, j: (0, j)),
            pl.BlockSpec((D_MODEL, tf), lambda i, j: (0, j)),
            pl.BlockSpec((tf, D_MODEL), lambda i, j: (j, 0)),
            *side_specs,
        ],
        out_specs=(pl.BlockSpec((tm, D_MODEL), lambda i, j: (i, 0)),
                   pl.BlockSpec((ns, D_MODEL), lambda i, j: (0, 0)),
                   *side_specs),
        scratch_shapes=[pltpu.VMEM((tm + ns, D_MODEL), BF16)],
    )
    return (head, rest), out_s, side_bf16


def _row_tile_specs(x, tm, tile_of):
    n_head = x[0].shape[0] // tm
    return (pl.BlockSpec((tm, D_MODEL), lambda *idx: (jnp.minimum(tile_of(*idx), n_head - 1), 0)),
            pl.BlockSpec((tm, D_MODEL), lambda *idx: (jnp.maximum(tile_of(*idx) - n_head, 0), 0)))


def _row_tile(tile, head_ref, rest_ref):
    n_head = FFN_TM // head_ref.shape[0]
    return jnp.where(tile < n_head, head_ref[...], rest_ref[...])


def _project(x, g_ref, w_ref, put_v, put_u):
    h = _rms(x, g_ref[...]).astype(BF16)
    nc = 512
    for c in range(CONV_DIM // nc):
        sl = slice(c * nc, (c + 1) * nc)
        a_val = jnp.dot(h, w_ref[:, sl], preferred_element_type=F32)
        a_gate = jnp.dot(h, w_ref[:, CONV_DIM + c * nc:CONV_DIM + (c + 1) * nc], preferred_element_type=F32)
        put_v(sl, a_val * jax.nn.sigmoid(a_gate))
        put_u(sl, jnp.dot(h, w_ref[:, 2 * CONV_DIM + c * nc:2 * CONV_DIM + (c + 1) * nc],
                          preferred_element_type=F32))


def _conv_post(acc, cb, lg, lb):
    y = acc + cb
    mu = jnp.mean(y, axis=-1, keepdims=True)
    yc = y - mu
    var = jnp.mean(yc * yc, axis=-1, keepdims=True)
    z = yc * lax.rsqrt(var + EPS) * lg + lb
    return z * jax.nn.sigmoid(z)


def _mix_tail(a_ref, d_ref, x_ref, pw_ref, ps_ref, wo_ref, o_ref):
    for g in range(len(POOL_WINDOWS)):
        sl = slice(g * POOL_GROUP_DIM, (g + 1) * POOL_GROUP_DIM)
        y = jnp.dot(d_ref[:, sl], pw_ref[g], preferred_element_type=F32) * ps_ref[:, sl]
        a_ref[:, CONV_DIM + g * POOL_GROUP_DIM:CONV_DIM + (g + 1) * POOL_GROUP_DIM] = y.astype(BF16)
    o_ref[...] = x_ref[...] + jnp.dot(a_ref[...], wo_ref[...], preferred_element_type=F32)


def _rows_above(tiles, b):
    rot = [pltpu.roll(t, SUBLANES - b, axis=0) for t in tiles]
    own = lax.broadcasted_iota(jnp.int32, tiles[0].shape, 0) < SUBLANES - b
    return [jnp.where(own, rot[j], rot[j + 1]) for j in range(len(tiles) - 1)]


def _rows_below(tiles, d):
    rot = [pltpu.roll(t, d, axis=0) for t in tiles]
    own = lax.broadcasted_iota(jnp.int32, tiles[0].shape, 0) >= d
    return [jnp.where(own, rot[j], rot[max(j - 1, 0)]) for j in range(len(tiles))]


def _conv_chunk(v_ref, wb_ref, r0, sl):
    n_out = ROW_CHUNK // SUBLANES
    n_src = n_out + CONV_HALO // SUBLANES
    src = [v_ref[r0 + SUBLANES * j:r0 + SUBLANES * (j + 1), sl] for j in range(n_src)]
    acc = [None] * n_out
    for b in range(SUBLANES):
        taps = [k for k in range(CONV_WIDTH) if (CONV_HALO - CONV_CTX + k) % SUBLANES == b]
        if not taps:
            continue
        shifted = src if b == 0 else _rows_above(src, b)
        for k in taps:
            a = (CONV_HALO - CONV_CTX + k) // SUBLANES
            wk = wb_ref[k, :, sl]
            for i in range(n_out):
                term = wk * shifted[a + i]
                acc[i] = term if acc[i] is None else acc[i] + term
    return jnp.concatenate(acc, axis=0)


def _pool_chunk(u_ref, r0, pos0, d_ref):
    n_out = ROW_CHUNK // SUBLANES
    n_ctx = POOL_HALO // SUBLANES
    for g, w in enumerate(POOL_WINDOWS):
        sl = slice(g * POOL_GROUP_DIM, (g + 1) * POOL_GROUP_DIM)
        cur = [u_ref[r0 + SUBLANES * j:r0 + SUBLANES * (j + 1), sl] for j in range(n_out + n_ctx)]
        s, span = cur, 1
        while span < w:
            if span < SUBLANES:
                below = _rows_below(s, span)
            else:
                below = [s[0]] + s[:-1]
            s = [x + y for x, y in zip(s, below)]
            span *= 2
        d = []
        for i in range(n_out):
            pos = pos0 + r0 + SUBLANES * i + lax.broadcasted_iota(jnp.int32, cur[0].shape, 0)
            cnt = jnp.minimum(pos + 1, w).astype(F32)
            d.append(s[n_ctx + i] / cnt - cur[n_ctx + i])
        d_ref[r0:r0 + ROW_CHUNK, sl] = jnp.concatenate(d, axis=0).astype(BF16)


def _mix_prompt_kernel(xh_ref, xr_ref, xs_ref, g_ref, w_ref, cw_ref, cb_ref, lg_ref, lb_ref, pw_ref, ps_ref,
                       a_ref, ctail_ref, ptail_ref, vs_ref, us_ref, vext_ref, uext_ref, d_ref, wb_ref,
                       *, tm, tiles_per_seq):
    s = pl.program_id(0)
    n_tiles = pl.num_programs(0) - 1
    slot = lax.rem(s, 2)
    v_new, u_new = vext_ref.at[slot], uext_ref.at[slot]
    v_old, u_old = vext_ref.at[1 - slot], uext_ref.at[1 - slot]

    @pl.when(s == 0)
    def _():
        vext_ref[1] = jnp.zeros(vext_ref.shape[1:], F32)
        uext_ref[1] = jnp.zeros(uext_ref.shape[1:], F32)
        for k in range(CONV_WIDTH):
            wb_ref[k] = jnp.broadcast_to(cw_ref[k:k + 1, :], wb_ref.shape[1:])

    def put_v(sl, val):
        v_new[CONV_HALO:, sl] = val

    def put_u(sl, val):
        u_new[POOL_HALO:, sl] = val

    tile = jnp.minimum(s, n_tiles - 1)
    _project(_row_tile(tile, xh_ref, xr_ref), g_ref, w_ref, put_v, put_u)
    starts_seq = lax.rem(tile, tiles_per_seq) == 0
    v_new[0:CONV_HALO, :] = jnp.where(starts_seq, 0.0, v_old[tm:tm + CONV_HALO, :])
    u_new[0:POOL_HALO, :] = jnp.where(starts_seq, 0.0, u_old[tm:tm + POOL_HALO, :])

    pos0 = lax.rem(jnp.maximum(s - 1, 0), tiles_per_seq) * tm
    cb, lg, lb = cb_ref[...], lg_ref[...], lb_ref[...]
    for r in range(tm // ROW_CHUNK):
        r0 = r * ROW_CHUNK
        conv = jnp.concatenate(
            [_conv_chunk(v_old, wb_ref, r0, slice(c * LANE_GROUP, (c + 1) * LANE_GROUP))
             for c in range(CONV_DIM // LANE_GROUP)], axis=1)
        a_ref[r0:r0 + ROW_CHUNK, 0:CONV_DIM] = _conv_post(conv, cb, lg, lb).astype(BF16)
        _pool_chunk(u_old, r0, pos0, d_ref)
    for g in range(len(POOL_WINDOWS)):
        sl = slice(g * POOL_GROUP_DIM, (g + 1) * POOL_GROUP_DIM)
        y = jnp.dot(d_ref[:, sl], pw_ref[g], preferred_element_type=F32) * ps_ref[:, sl]
        a_ref[:, CONV_DIM + g * POOL_GROUP_DIM:CONV_DIM + (g + 1) * POOL_GROUP_DIM] = y.astype(BF16)
    ctail_ref[0] = v_old[tm:tm + CONV_HALO, :]
    ptail_ref[0] = u_old[tm:tm + POOL_HALO, :]

    @pl.when(s == n_tiles)
    def _():
        def put_vs(sl, val):
            vs_ref[:, sl] = val

        def put_us(sl, val):
            us_ref[:, sl] = val

        _project(xs_ref[...], g_ref, w_ref, put_vs, put_us)


def _mix_prompt(x, xs, g, w_in, cw, cb, lg, lb, pw, ps, *, batch, seq, tm):
    tiles_per_seq = seq // tm
    n_tiles = batch * tiles_per_seq
    ns = xs.shape[0]
    prev = lambda s: (jnp.maximum(s - 1, 0), 0)
    prev_seq = lambda s: (jnp.maximum(s - 1, 0) // tiles_per_seq, 0, 0)
    return _call(
        functools.partial(_mix_prompt_kernel, tm=tm, tiles_per_seq=tiles_per_seq),
        *x, xs, g, w_in, cw, cb, lg, lb, pw, ps,
        name="mix_prompt", semantics=("arbitrary",),
        out_shape=(jax.ShapeDtypeStruct((batch * seq, D_MODEL), BF16),
                   jax.ShapeDtypeStruct((batch, CONV_HALO, CONV_DIM), F32),
                   jax.ShapeDtypeStruct((batch, POOL_HALO, POOL_DIM), F32),
                   jax.ShapeDtypeStruct((ns, CONV_DIM), F32),
                   jax.ShapeDtypeStruct((ns, POOL_DIM), F32)),
        grid=(n_tiles + 1,),
        in_specs=[
            *_row_tile_specs(x, tm, lambda s: jnp.minimum(s, n_tiles - 1)), _resident(xs.shape),
            _resident(g.shape), _resident(w_in.shape), _resident(cw.shape), _resident(cb.shape),
            _resident(lg.shape), _resident(lb.shape), _resident(pw.shape), _resident(ps.shape),
        ],
        out_specs=(pl.BlockSpec((tm, D_MODEL), prev),
                   pl.BlockSpec((1, CONV_HALO, CONV_DIM), prev_seq),
                   pl.BlockSpec((1, POOL_HALO, POOL_DIM), prev_seq),
                   pl.BlockSpec((ns, CONV_DIM), lambda s: (0, 0)),
                   pl.BlockSpec((ns, POOL_DIM), lambda s: (0, 0))),
        scratch_shapes=[
            pltpu.VMEM((2, CONV_HALO + tm, CONV_DIM), F32),
            pltpu.VMEM((2, POOL_HALO + tm, POOL_DIM), F32),
            pltpu.VMEM((tm, POOL_DIM), BF16),
            pltpu.VMEM((CONV_WIDTH, SUBLANES, CONV_DIM), F32),
        ],
    )


def _out_proj_kernel(xh_ref, xr_ref, a_ref, wo_ref, o_ref):
    x = _row_tile(pl.program_id(0), xh_ref, xr_ref)
    o_ref[...] = x + jnp.dot(a_ref[...], wo_ref[...], preferred_element_type=F32)


def _out_proj(x, a, wo, *, tm):
    rows = a.shape[0]
    return _call(
        _out_proj_kernel, *x, a, wo,
        name="out_proj", semantics=("parallel",),
        out_shape=jax.ShapeDtypeStruct((rows, D_MODEL), F32),
        grid=(rows // tm,),
        in_specs=[*_row_tile_specs(x, tm, lambda i: i),
                  pl.BlockSpec((tm, D_MODEL), lambda i: (i, 0)),
                  _resident(wo.shape)],
        out_specs=pl.BlockSpec((tm, D_MODEL), lambda i: (i, 0)),
    )


def _mix_out_sample_kernel(v_ref, u_ref, sc_ref, sc_next_ref, sp_ref, sp_next_ref, x_ref, cw_ref, cb_ref, lg_ref, lb_ref,
                           pw_ref, ps_ref, wo_ref, o_ref, nsc_ref, nsp_ref, acc_ref, s_ref, a_ref, d_ref):
    c = pl.program_id(0)
    last = c == pl.num_programs(0) - 1

    @pl.when(c == 0)
    def _():
        acc_ref[...] = jnp.zeros(acc_ref.shape, F32)
        s_ref[...] = jnp.zeros(s_ref.shape, F32)

    acc = acc_ref[...]
    for r in range(CONV_STEP_ROWS):
        acc = acc + cw_ref[pl.ds(c * CONV_STEP_ROWS + r, 1), :] * sc_ref[r]
    acc_ref[...] = acc
    for g, w in enumerate(POOL_WINDOWS):
        sl = slice(g * POOL_GROUP_DIM, (g + 1) * POOL_GROUP_DIM)
        s = s_ref[:, sl]
        for r in range(POOL_STEP_ROWS):
            s = s + jnp.where(c * POOL_STEP_ROWS + r >= POOL_CTX + 1 - w, sp_ref[r, :, sl], 0.0)
        s_ref[:, sl] = s

    for r in range(CONV_STEP_ROWS - 1):
        nsc_ref[r] = sc_ref[r + 1]
    nsc_ref[CONV_STEP_ROWS - 1] = jnp.where(last, v_ref[...], sc_next_ref[0])
    for r in range(POOL_STEP_ROWS - 1):
        nsp_ref[r] = sp_ref[r + 1]
    nsp_ref[POOL_STEP_ROWS - 1] = jnp.where(last, u_ref[...], sp_next_ref[0])

    @pl.when(last)
    def _():
        conv = acc_ref[...] + cw_ref[CONV_CTX:CONV_WIDTH, :] * v_ref[...]
        a_ref[:, 0:CONV_DIM] = _conv_post(conv, cb_ref[...], lg_ref[...], lb_ref[...]).astype(BF16)
        u = u_ref[...]
        s = s_ref[...] + u
        lane1 = lax.broadcasted_iota(jnp.int32, (1, POOL_DIM), 1)
        cnt = jnp.zeros((1, POOL_DIM), F32)
        for g, w in enumerate(POOL_WINDOWS):
            cnt = jnp.where(lane1 // POOL_GROUP_DIM == g, float(min(PAST_LEN + 1, w)), cnt)
        d_ref[...] = (s / cnt - u).astype(BF16)
        _mix_tail(a_ref, d_ref, x_ref, pw_ref, ps_ref, wo_ref, o_ref)


def _mix_out_sample(v, u, sc, sp, x, cw, cb, lg, lb, pw, ps, wo):
    nb = v.shape[0]
    whole = (v, u, x, cw, cb, lg, lb, pw, ps, wo)
    v_s, u_s, x_s, *w_s = [_resident(a.shape) for a in whole]
    rows = lambda n, index: pl.BlockSpec((n, nb, sc.shape[2]), lambda c: (index(c), 0, 0))
    conv_rows = rows(CONV_STEP_ROWS, lambda c: c)
    conv_next = rows(1, lambda c: jnp.minimum((c + 1) * CONV_STEP_ROWS, CONV_CTX - 1))
    pool_rows = rows(POOL_STEP_ROWS, lambda c: c)
    pool_next = rows(1, lambda c: jnp.minimum((c + 1) * POOL_STEP_ROWS, POOL_CTX - 1))
    return _call(
        _mix_out_sample_kernel, v, u, sc, sc, sp, sp, x, cw, cb, lg, lb, pw, ps, wo,
        name="mix_out_sample", semantics=("arbitrary",),
        out_shape=(jax.ShapeDtypeStruct((nb, D_MODEL), F32),
                   jax.ShapeDtypeStruct(sc.shape, F32), jax.ShapeDtypeStruct(sp.shape, F32)),
        grid=(SAMPLE_STEPS,),
        in_specs=[v_s, u_s, conv_rows, conv_next, pool_rows, pool_next, x_s, *w_s],
        out_specs=(pl.BlockSpec((nb, D_MODEL), lambda c: (0, 0)), conv_rows, pool_rows),
        scratch_shapes=[
            pltpu.VMEM((nb, CONV_DIM), F32),
            pltpu.VMEM((nb, POOL_DIM), F32),
            pltpu.VMEM((nb, D_MODEL), BF16),
            pltpu.VMEM((nb, POOL_DIM), BF16),
        ],
    )


def _ple_rows(x, p_ref, gp_ref, wg_ref, wp_ref, gf_ref, o_ref):
    r = _rms(x, gp_ref[...]).astype(BF16)
    p = p_ref[...].astype(BF16)
    nc = 512
    for c in range(D_MODEL // nc):
        sl = slice(c * nc, (c + 1) * nc)
        gate = jax.nn.sigmoid(jnp.dot(r, wg_ref[:, sl], preferred_element_type=F32))
        proj = jnp.dot(p, wp_ref[:, sl], preferred_element_type=F32)
        o_ref[:, sl] = x[:, sl] + gate * proj
    o_ref[...] = _rms(o_ref[...], gf_ref[...])


def _ple_kernel(xh_ref, xr_ref, p_ref, xs_ref, psm_ref, gp_ref, wg_ref, wp_ref, gf_ref, o_ref, os_ref):
    _ple_rows(_row_tile(pl.program_id(0), xh_ref, xr_ref), p_ref, gp_ref, wg_ref, wp_ref, gf_ref, o_ref)

    @pl.when(pl.program_id(0) == pl.num_programs(0) - 1)
    def _():
        _ple_rows(xs_ref[...], psm_ref, gp_ref, wg_ref, wp_ref, gf_ref, os_ref)


def _ple(x, p, xs, ps, gp, wg, wp, gf, *, tm):
    rows, ns = p.shape[0], xs.shape[0]
    return _call(
        _ple_kernel, *x, p, xs, ps, gp, wg, wp, gf,
        name="ple", semantics=("arbitrary",),
        out_shape=(jax.ShapeDtypeStruct((rows, D_MODEL), F32), jax.ShapeDtypeStruct((ns, D_MODEL), F32)),
        grid=(rows // tm,),
        in_specs=[
            *_row_tile_specs(x, tm, lambda i: i),
            pl.BlockSpec((tm, PLE_DIM), lambda i: (i, 0)),
            _resident(xs.shape), _resident(ps.shape),
            _resident((1, D_MODEL)), _resident(wg.shape), _resident(wp.shape), _resident((1, D_MODEL)),
        ],
        out_specs=(pl.BlockSpec((tm, D_MODEL), lambda i: (i, 0)),
                   pl.BlockSpec((ns, D_MODEL), lambda i: (0, 0))),
    )


def kernel(x_prompt, x_sample, state_conv, state_pool, p_prompt, p_sample, norm_ffn1, w_ffn1_in, w_ffn1_out, norm_mix, w_in, conv_w, conv_b, conv_ln_g, conv_ln_b, pool_w, pool_scale, w_out, norm_ffn2, w_ffn2_in, w_ffn2_out, norm_ple, w_ple_gate, w_ple_proj, norm_final):
    assert norm_ffn1.shape[0] == 1, "the final norm is fused into the layer's last stage: one layer only"
    batch, seq, _ = x_prompt.shape
    nb = x_sample.shape[0]
    xp = x_prompt.reshape(batch * seq, D_MODEL)
    xs = x_sample.reshape(nb, D_MODEL)
    row = lambda a: a.reshape(1, -1)

    g1, gm, g2, gp, gf = row(norm_ffn1[0]), row(norm_mix[0]), row(norm_ffn2[0]), row(norm_ple[0]), row(norm_final)
    pool_w2d = pool_w[0].reshape(len(POOL_WINDOWS) * POOL_GROUP_DIM, POOL_GROUP_DIM)

    x1, x1s, (wi, wo, pw) = _ffn(xp, xs, g1, w_ffn1_in[0], w_ffn1_out[0], tm=FFN_TM, tf=FFN_TF,
                                 side=(w_in[0], w_out[0], pool_w2d))
    pw = pw.reshape(pool_w[0].shape)
    mix = (conv_w[0], row(conv_b[0]), row(conv_ln_g[0]), row(conv_ln_b[0]), pw, row(pool_scale[0]), wo)
    a, v_tail, u_tail, vs, us = _mix_prompt(x1, x1s, gm, wi, *mix[:-1], batch=batch, seq=seq, tm=PROMPT_TM)
    x2 = _out_proj(x1, a, wo, tm=PROMPT_TM)
    to_ctx_major = lambda st: jnp.transpose(st, (1, 0, 2))
    x2s, conv_t, pool_t = _mix_out_sample(vs, us, to_ctx_major(state_conv[0]), to_ctx_major(state_pool[0]), x1s, *mix)
    new_conv_s, new_pool_s = jnp.transpose(conv_t, (1, 0, 2)), jnp.transpose(pool_t, (1, 0, 2))
    x3, x3s, (wg, wp) = _ffn(x2, x2s, g2, w_ffn2_in[0], w_ffn2_out[0], tm=FFN_TM, tf=FFN_TF,
                             side=(w_ple_gate[0], w_ple_proj[0]))
    yp, ys = _ple(x3, p_prompt[0].reshape(batch * seq, PLE_DIM), x3s, p_sample[0].reshape(nb, PLE_DIM),
                  gp, wg, wp, gf, tm=PROMPT_TM)
    new_conv_p = v_tail[:, CONV_HALO - CONV_CTX:]
    new_pool_p = u_tail[:, POOL_HALO - POOL_CTX:]

    return (yp.reshape(batch, seq, D_MODEL), ys.reshape(nb, 1, D_MODEL),
            new_conv_p[None], new_conv_s[None], new_pool_p[None], new_pool_s[None])
```

```python
import functools
import math

import jax
import jax.numpy as jnp
from jax import lax
from jax.experimental import pallas as pl
from jax.experimental.pallas import tpu as pltpu

D_MODEL = 2048
D_FF = 5632
CONV_DIM = 1024
POOL_DIM = 1024
POOL_WINDOWS = (2, 4, 8, 16)
POOL_GROUP_DIM = POOL_DIM // len(POOL_WINDOWS)
POOL_CTX = max(POOL_WINDOWS) - 1
CONV_WIDTH = 31
CONV_CTX = CONV_WIDTH - 1
PLE_DIM = 256
PAST_LEN = 16384
EPS = 1e-6

F32 = jnp.float32
BF16 = jnp.bfloat16

V7X_VMEM_BYTES = 64 * 1024 * 1024
VMEM_HEADROOM_BYTES = 2 * 1024 * 1024
VMEM_TEMP_BYTES = 16 * 1024 * 1024
PROMPT_TM = 512
FFN_TM = 1024
FFN_TF = 512
CONV_HALO = 32
POOL_HALO = 16
ROW_CHUNK = 64
LANE_GROUP = 256
SAMPLE_STEPS = 5
CONV_STEP_ROWS = CONV_CTX // SAMPLE_STEPS
POOL_STEP_ROWS = POOL_CTX // SAMPLE_STEPS
SIDE_BLOCK_ROWS = 32
SUBLANES = 8


def _rms(x, g):
    ms = jnp.mean(x * x, axis=-1, keepdims=True)
    return x * lax.rsqrt(ms + EPS) * g


def _window_bytes(spec, dtype):
    if spec.block_shape is None:
        return 0
    buffers = 1 if spec.pipeline_mode is not None else 2
    return math.prod(spec.block_shape) * jnp.dtype(dtype).itemsize * buffers


def _call(kernel, *args, name, grid, in_specs, out_specs, out_shape, semantics, scratch_shapes=(), **kwargs):
    outs, ospecs = (out_shape, out_specs) if isinstance(out_shape, tuple) else ((out_shape,), (out_specs,))
    need = (sum(_window_bytes(sp, a.dtype) for sp, a in zip(in_specs, args))
            + sum(_window_bytes(sp, o.dtype) for sp, o in zip(ospecs, outs))
            + sum(math.prod(sc.shape) * jnp.dtype(sc.dtype).itemsize for sc in scratch_shapes))
    limit = min(need + VMEM_TEMP_BYTES, V7X_VMEM_BYTES - VMEM_HEADROOM_BYTES)
    call = pl.pallas_call(
        kernel, out_shape=out_shape, grid=grid, in_specs=in_specs, out_specs=out_specs,
        scratch_shapes=list(scratch_shapes), name=name,
        compiler_params=pltpu.CompilerParams(dimension_semantics=semantics, vmem_limit_bytes=limit), **kwargs)
    return call(*args)


def _resident(shape):
    nd = len(shape)
    return pl.BlockSpec(shape, lambda *_: (0,) * nd, pipeline_mode=pl.Buffered(1))


def _ffn_rows(j, x_refs, g_ref, w_refs, o_refs, h_ref, bf16_copy_refs=(None, None, None)):
    starts = [sum(x.shape[0] for x in x_refs[:k]) for k in range(len(x_refs) + 1)]

    @pl.when(j == 0)
    def _():
        for x_ref, o_ref, lo, hi in zip(x_refs, o_refs, starts, starts[1:]):
            x = x_ref[...]
            h_ref[lo:hi, :] = _rms(x, g_ref[...]).astype(BF16)
            o_ref[...] = x

    def weight(k):
        w = w_refs[k][...].astype(BF16)
        if bf16_copy_refs[k] is not None:
            bf16_copy_refs[k][...] = w
        return w

    h = h_ref[...]
    gate = jnp.dot(h, weight(0), preferred_element_type=F32)
    up = jnp.dot(h, weight(1), preferred_element_type=F32)
    act = (gate * jax.nn.sigmoid(gate) * up * 0.5).astype(BF16)
    delta = jnp.dot(act, weight(2), preferred_element_type=F32)
    for o_ref, lo, hi in zip(o_refs, starts, starts[1:]):
        o_ref[...] += delta[lo:hi]


def _ffn_head_kernel(xp_ref, xs_ref, g_ref, wg_ref, wu_ref, wo_ref, op_ref, os_ref, wgb_ref, wub_ref, wob_ref, h_ref):
    _ffn_rows(pl.program_id(0), (xp_ref, xs_ref), g_ref, (wg_ref, wu_ref, wo_ref), (op_ref, os_ref), h_ref,
              bf16_copy_refs=(wgb_ref, wub_ref, wob_ref))


def _ffn_body_kernel(*refs, n_side):
    x_ref, g_ref, wg_ref, wu_ref, wo_ref = refs[:5]
    side_in = refs[5:5 + n_side]
    o_ref = refs[5 + n_side]
    side_out = refs[6 + n_side:6 + 2 * n_side]
    h_ref = refs[6 + 2 * n_side]
    for src, dst in zip(side_in, side_out):
        dst[...] = src[...].astype(BF16)
    _ffn_rows(pl.program_id(1), (x_ref,), g_ref, (wg_ref, wu_ref, wo_ref), (o_ref,), h_ref)


def _ffn(xp, xs, g, w_in, w_out, *, tm, tf, side=()):
    ns = xs.shape[0]
    nf = D_FF // tf
    hf = tf // 2
    nh = D_FF // hf
    once = dict(pipeline_mode=pl.Buffered(1))
    head, out_s, wg, wu, wo = _call(
        _ffn_head_kernel, xp, xs, g, w_in, w_in, w_out,
        name="ffn_head", semantics=("arbitrary",),
        out_shape=(jax.ShapeDtypeStruct((tm, D_MODEL), F32),
                   jax.ShapeDtypeStruct(xs.shape, F32),
                   jax.ShapeDtypeStruct((D_MODEL, D_FF), BF16),
                   jax.ShapeDtypeStruct((D_MODEL, D_FF), BF16),
                   jax.ShapeDtypeStruct((D_FF, D_MODEL), BF16)),
        grid=(nh,),
        in_specs=[
            pl.BlockSpec((tm, D_MODEL), lambda j: (0, 0), **once),
            pl.BlockSpec((ns, D_MODEL), lambda j: (0, 0), **once),
            pl.BlockSpec((1, D_MODEL), lambda j: (0, 0), **once),
            pl.BlockSpec((D_MODEL, hf), lambda j: (0, j)),
            pl.BlockSpec((D_MODEL, hf), lambda j: (0, j + nh)),
            pl.BlockSpec((hf, D_MODEL), lambda j: (j, 0)),
        ],
        out_specs=(pl.BlockSpec((tm, D_MODEL), lambda j: (0, 0)),
                   pl.BlockSpec((ns, D_MODEL), lambda j: (0, 0)),
                   pl.BlockSpec((D_MODEL, hf), lambda j: (0, j)),
                   pl.BlockSpec((D_MODEL, hf), lambda j: (0, j)),
                   pl.BlockSpec((hf, D_MODEL), lambda j: (j, 0))),
        scratch_shapes=[pltpu.VMEM((tm + ns, D_MODEL), BF16)],
    )

    n_body_tiles = xp.shape[0] // tm - 1
    side_rows = [max(SIDE_BLOCK_ROWS, -(-a.shape[0] // (n_body_tiles * nf))) for a in side]
    assert all(a.shape[0] % r == 0 for a, r in zip(side, side_rows))

    def side_spec(a, r):
        last = a.shape[0] // r - 1
        return pl.BlockSpec((r, a.shape[1]), lambda i, j: (jnp.minimum(i * nf + j, last), 0))

    side_specs = [side_spec(a, r) for a, r in zip(side, side_rows)]
    rest, *side_bf16 = _call(
        functools.partial(_ffn_body_kernel, n_side=len(side)), xp, g, wg, wu, wo, *side,
        name="ffn_body", semantics=("arbitrary", "arbitrary"),
        out_shape=(jax.ShapeDtypeStruct((xp.shape[0] - tm, D_MODEL), F32),
                   *[jax.ShapeDtypeStruct(a.shape, BF16) for a in side]),
        grid=(n_body_tiles, nf),
        in_specs=[
            pl.BlockSpec((tm, D_MODEL), lambda i, j: (i + 1, 0)),
            pl.BlockSpec((1, D_MODEL), lambda i, j: (0, 0), **once),
            pl.BlockSpec((D_MODEL, tf), lambda i, j: (0, j)),
            pl.BlockSpec((D_MODEL, tf), lambda i, j: (0, j)),
            pl.BlockSpec((tf, D_MODEL), lambda i, j: (j, 0)),
            *side_specs,
        ],
        out_specs=(pl.BlockSpec((tm, D_MODEL), lambda i, j: (i, 0)), *side_specs),
        scratch_shapes=[pltpu.VMEM((tm, D_MODEL), BF16)],
    )
    return (head, rest), out_s, side_bf16


def _row_tile_specs(x, tm, tile_of):
    n_head = x[0].shape[0] // tm
    return (pl.BlockSpec((tm, D_MODEL), lambda *idx: (jnp.minimum(tile_of(*idx), n_head - 1), 0)),
            pl.BlockSpec((tm, D_MODEL), lambda *idx: (jnp.maximum(tile_of(*idx) - n_head, 0), 0)))


def _row_tile(tile, head_ref, rest_ref):
    n_head = FFN_TM // head_ref.shape[0]
    return jnp.where(tile < n_head, head_ref[...], rest_ref[...])


def _project(x, g_ref, w_ref, put_v, put_u):
    h = _rms(x, g_ref[...]).astype(BF16)
    nc = 512
    for c in range(CONV_DIM // nc):
        sl = slice(c * nc, (c + 1) * nc)
        a_val = jnp.dot(h, w_ref[:, sl], preferred_element_type=F32)
        a_gate = jnp.dot(h, w_ref[:, CONV_DIM + c * nc:CONV_DIM + (c + 1) * nc], preferred_element_type=F32)
        put_v(sl, a_val * jax.nn.sigmoid(a_gate))
        put_u(sl, jnp.dot(h, w_ref[:, 2 * CONV_DIM + c * nc:2 * CONV_DIM + (c + 1) * nc],
                          preferred_element_type=F32))


def _conv_post(acc, cb, lg, lb):
    y = acc + cb
    mu = jnp.mean(y, axis=-1, keepdims=True)
    yc = y - mu
    var = jnp.mean(yc * yc, axis=-1, keepdims=True)
    z = yc * lax.rsqrt(var + EPS) * lg + lb
    return z * jax.nn.sigmoid(z)


def _mix_tail(a_ref, d_ref, x_ref, pw_ref, ps_ref, wo_ref, o_ref):
    for g in range(len(POOL_WINDOWS)):
        sl = slice(g * POOL_GROUP_DIM, (g + 1) * POOL_GROUP_DIM)
        y = jnp.dot(d_ref[:, sl], pw_ref[g], preferred_element_type=F32) * ps_ref[:, sl]
        a_ref[:, CONV_DIM + g * POOL_GROUP_DIM:CONV_DIM + (g + 1) * POOL_GROUP_DIM] = y.astype(BF16)
    o_ref[...] = x_ref[...] + jnp.dot(a_ref[...], wo_ref[...], preferred_element_type=F32)


def _rows_above(tiles, b):
    rot = [pltpu.roll(t, SUBLANES - b, axis=0) for t in tiles]
    own = lax.broadcasted_iota(jnp.int32, tiles[0].shape, 0) < SUBLANES - b
    return [jnp.where(own, rot[j], rot[j + 1]) for j in range(len(tiles) - 1)]


def _rows_below(tiles, d):
    rot = [pltpu.roll(t, d, axis=0) for t in tiles]
    own = lax.broadcasted_iota(jnp.int32, tiles[0].shape, 0) >= d
    return [jnp.where(own, rot[j], rot[max(j - 1, 0)]) for j in range(len(tiles))]


def _conv_chunk(v_ref, wb_ref, r0, sl):
    n_out = ROW_CHUNK // SUBLANES
    n_src = n_out + CONV_HALO // SUBLANES
    src = [v_ref[r0 + SUBLANES * j:r0 + SUBLANES * (j + 1), sl] for j in range(n_src)]
    acc = [None] * n_out
    for b in range(SUBLANES):
        taps = [k for k in range(CONV_WIDTH) if (CONV_HALO - CONV_CTX + k) % SUBLANES == b]
        if not taps:
            continue
        shifted = src if b == 0 else _rows_above(src, b)
        for k in taps:
            a = (CONV_HALO - CONV_CTX + k) // SUBLANES
            wk = wb_ref[k, :, sl]
            for i in range(n_out):
                term = wk * shifted[a + i]
                acc[i] = term if acc[i] is None else acc[i] + term
    return jnp.concatenate(acc, axis=0)


def _pool_chunk(u_ref, r0, pos0, d_ref):
    n_out = ROW_CHUNK // SUBLANES
    n_ctx = POOL_HALO // SUBLANES
    for g, w in enumerate(POOL_WINDOWS):
        sl = slice(g * POOL_GROUP_DIM, (g + 1) * POOL_GROUP_DIM)
        cur = [u_ref[r0 + SUBLANES * j:r0 + SUBLANES * (j + 1), sl] for j in range(n_out + n_ctx)]
        s, span = cur, 1
        while span < w:
            if span < SUBLANES:
                below = _rows_below(s, span)
            else:
                below = [s[0]] + s[:-1]
            s = [x + y for x, y in zip(s, below)]
            span *= 2
        d = []
        for i in range(n_out):
            pos = pos0 + r0 + SUBLANES * i + lax.broadcasted_iota(jnp.int32, cur[0].shape, 0)
            cnt = jnp.minimum(pos + 1, w).astype(F32)
            d.append(s[n_ctx + i] / cnt - cur[n_ctx + i])
        d_ref[r0:r0 + ROW_CHUNK, sl] = jnp.concatenate(d, axis=0).astype(BF16)


def _mix_prompt_kernel(xh_ref, xr_ref, xs_ref, g_ref, w_ref, cw_ref, cb_ref, lg_ref, lb_ref, pw_ref, ps_ref,
                       a_ref, ctail_ref, ptail_ref, vs_ref, us_ref, vext_ref, uext_ref, d_ref, wb_ref,
                       *, tm, tiles_per_seq):
    s = pl.program_id(0)
    n_tiles = pl.num_programs(0) - 1
    slot = lax.rem(s, 2)
    v_new, u_new = vext_ref.at[slot], uext_ref.at[slot]
    v_old, u_old = vext_ref.at[1 - slot], uext_ref.at[1 - slot]

    @pl.when(s == 0)
    def _():
        vext_ref[1] = jnp.zeros(vext_ref.shape[1:], F32)
        uext_ref[1] = jnp.zeros(uext_ref.shape[1:], F32)
        for k in range(CONV_WIDTH):
            wb_ref[k] = jnp.broadcast_to(cw_ref[k:k + 1, :], wb_ref.shape[1:])

    def put_v(sl, val):
        v_new[CONV_HALO:, sl] = val

    def put_u(sl, val):
        u_new[POOL_HALO:, sl] = val

    tile = jnp.minimum(s, n_tiles - 1)
    _project(_row_tile(tile, xh_ref, xr_ref), g_ref, w_ref, put_v, put_u)
    starts_seq = lax.rem(tile, tiles_per_seq) == 0
    v_new[0:CONV_HALO, :] = jnp.where(starts_seq, 0.0, v_old[tm:tm + CONV_HALO, :])
    u_new[0:POOL_HALO, :] = jnp.where(starts_seq, 0.0, u_old[tm:tm + POOL_HALO, :])

    pos0 = lax.rem(jnp.maximum(s - 1, 0), tiles_per_seq) * tm
    cb, lg, lb = cb_ref[...], lg_ref[...], lb_ref[...]
    for r in range(tm // ROW_CHUNK):
        r0 = r * ROW_CHUNK
        conv = jnp.concatenate(
            [_conv_chunk(v_old, wb_ref, r0, slice(c * LANE_GROUP, (c + 1) * LANE_GROUP))
             for c in range(CONV_DIM // LANE_GROUP)], axis=1)
        a_ref[r0:r0 + ROW_CHUNK, 0:CONV_DIM] = _conv_post(conv, cb, lg, lb).astype(BF16)
        _pool_chunk(u_old, r0, pos0, d_ref)
    for g in range(len(POOL_WINDOWS)):
        sl = slice(g * POOL_GROUP_DIM, (g + 1) * POOL_GROUP_DIM)
        y = jnp.dot(d_ref[:, sl], pw_ref[g], preferred_element_type=F32) * ps_ref[:, sl]
        a_ref[:, CONV_DIM + g * POOL_GROUP_DIM:CONV_DIM + (g + 1) * POOL_GROUP_DIM] = y.astype(BF16)
    ctail_ref[0] = v_old[tm:tm + CONV_HALO, :]
    ptail_ref[0] = u_old[tm:tm + POOL_HALO, :]

    @pl.when(s == n_tiles)
    def _():
        def put_vs(sl, val):
            vs_ref[:, sl] = val

        def put_us(sl, val):
            us_ref[:, sl] = val

        _project(xs_ref[...], g_ref, w_ref, put_vs, put_us)


def _mix_prompt(x, xs, g, w_in, cw, cb, lg, lb, pw, ps, *, batch, seq, tm):
    tiles_per_seq = seq // tm
    n_tiles = batch * tiles_per_seq
    ns = xs.shape[0]
    prev = lambda s: (jnp.maximum(s - 1, 0), 0)
    prev_seq = lambda s: (jnp.maximum(s - 1, 0) // tiles_per_seq, 0, 0)
    return _call(
        functools.partial(_mix_prompt_kernel, tm=tm, tiles_per_seq=tiles_per_seq),
        *x, xs, g, w_in, cw, cb, lg, lb, pw, ps,
        name="mix_prompt", semantics=("arbitrary",),
        out_shape=(jax.ShapeDtypeStruct((batch * seq, D_MODEL), BF16),
                   jax.ShapeDtypeStruct((batch, CONV_HALO, CONV_DIM), F32),
                   jax.ShapeDtypeStruct((batch, POOL_HALO, POOL_DIM), F32),
                   jax.ShapeDtypeStruct((ns, CONV_DIM), F32),
                   jax.ShapeDtypeStruct((ns, POOL_DIM), F32)),
        grid=(n_tiles + 1,),
        in_specs=[
            *_row_tile_specs(x, tm, lambda s: jnp.minimum(s, n_tiles - 1)), _resident(xs.shape),
            _resident(g.shape), _resident(w_in.shape), _resident(cw.shape), _resident(cb.shape),
            _resident(lg.shape), _resident(lb.shape), _resident(pw.shape), _resident(ps.shape),
        ],
        out_specs=(pl.BlockSpec((tm, D_MODEL), prev),
                   pl.BlockSpec((1, CONV_HALO, CONV_DIM), prev_seq),
                   pl.BlockSpec((1, POOL_HALO, POOL_DIM), prev_seq),
                   pl.BlockSpec((ns, CONV_DIM), lambda s: (0, 0)),
                   pl.BlockSpec((ns, POOL_DIM), lambda s: (0, 0))),
        scratch_shapes=[
            pltpu.VMEM((2, CONV_HALO + tm, CONV_DIM), F32),
            pltpu.VMEM((2, POOL_HALO + tm, POOL_DIM), F32),
            pltpu.VMEM((tm, POOL_DIM), BF16),
            pltpu.VMEM((CONV_WIDTH, SUBLANES, CONV_DIM), F32),
        ],
    )


def _out_proj_kernel(xh_ref, xr_ref, a_ref, wo_ref, o_ref):
    x = _row_tile(pl.program_id(0), xh_ref, xr_ref)
    o_ref[...] = x + jnp.dot(a_ref[...], wo_ref[...], preferred_element_type=F32)


def _out_proj(x, a, wo, *, tm):
    rows = a.shape[0]
    return _call(
        _out_proj_kernel, *x, a, wo,
        name="out_proj", semantics=("parallel",),
        out_shape=jax.ShapeDtypeStruct((rows, D_MODEL), F32),
        grid=(rows // tm,),
        in_specs=[*_row_tile_specs(x, tm, lambda i: i),
                  pl.BlockSpec((tm, D_MODEL), lambda i: (i, 0)),
                  _resident(wo.shape)],
        out_specs=pl.BlockSpec((tm, D_MODEL), lambda i: (i, 0)),
    )


def _mix_out_sample_kernel(v_ref, u_ref, sc_ref, sc_next_ref, sp_ref, sp_next_ref, x_ref, cw_ref, cb_ref, lg_ref, lb_ref,
                           pw_ref, ps_ref, wo_ref, o_ref, nsc_ref, nsp_ref, acc_ref, s_ref, a_ref, d_ref):
    c = pl.program_id(0)
    last = c == pl.num_programs(0) - 1

    @pl.when(c == 0)
    def _():
        acc_ref[...] = jnp.zeros(acc_ref.shape, F32)
        s_ref[...] = jnp.zeros(s_ref.shape, F32)

    acc = acc_ref[...]
    for r in range(CONV_STEP_ROWS):
        acc = acc + cw_ref[pl.ds(c * CONV_STEP_ROWS + r, 1), :] * sc_ref[r]
    acc_ref[...] = acc
    for g, w in enumerate(POOL_WINDOWS):
        sl = slice(g * POOL_GROUP_DIM, (g + 1) * POOL_GROUP_DIM)
        s = s_ref[:, sl]
        for r in range(POOL_STEP_ROWS):
            s = s + jnp.where(c * POOL_STEP_ROWS + r >= POOL_CTX + 1 - w, sp_ref[r, :, sl], 0.0)
        s_ref[:, sl] = s

    for r in range(CONV_STEP_ROWS - 1):
        nsc_ref[r] = sc_ref[r + 1]
    nsc_ref[CONV_STEP_ROWS - 1] = jnp.where(last, v_ref[...], sc_next_ref[0])
    for r in range(POOL_STEP_ROWS - 1):
        nsp_ref[r] = sp_ref[r + 1]
    nsp_ref[POOL_STEP_ROWS - 1] = jnp.where(last, u_ref[...], sp_next_ref[0])

    @pl.when(last)
    def _():
        conv = acc_ref[...] + cw_ref[CONV_CTX:CONV_WIDTH, :] * v_ref[...]
        a_ref[:, 0:CONV_DIM] = _conv_post(conv, cb_ref[...], lg_ref[...], lb_ref[...]).astype(BF16)
        u = u_ref[...]
        s = s_ref[...] + u
        lane1 = lax.broadcasted_iota(jnp.int32, (1, POOL_DIM), 1)
        cnt = jnp.zeros((1, POOL_DIM), F32)
        for g, w in enumerate(POOL_WINDOWS):
            cnt = jnp.where(lane1 // POOL_GROUP_DIM == g, float(min(PAST_LEN + 1, w)), cnt)
        d_ref[...] = (s / cnt - u).astype(BF16)
        _mix_tail(a_ref, d_ref, x_ref, pw_ref, ps_ref, wo_ref, o_ref)


def _mix_out_sample(v, u, sc, sp, x, cw, cb, lg, lb, pw, ps, wo):
    nb = v.shape[0]
    whole = (v, u, x, cw, cb, lg, lb, pw, ps, wo)
    v_s, u_s, x_s, *w_s = [_resident(a.shape) for a in whole]
    rows = lambda n, index: pl.BlockSpec((n, nb, sc.shape[2]), lambda c: (index(c), 0, 0))
    conv_rows = rows(CONV_STEP_ROWS, lambda c: c)
    conv_next = rows(1, lambda c: jnp.minimum((c + 1) * CONV_STEP_ROWS, CONV_CTX - 1))
    pool_rows = rows(POOL_STEP_ROWS, lambda c: c)
    pool_next = rows(1, lambda c: jnp.minimum((c + 1) * POOL_STEP_ROWS, POOL_CTX - 1))
    return _call(
        _mix_out_sample_kernel, v, u, sc, sc, sp, sp, x, cw, cb, lg, lb, pw, ps, wo,
        name="mix_out_sample", semantics=("arbitrary",),
        out_shape=(jax.ShapeDtypeStruct((nb, D_MODEL), F32),
                   jax.ShapeDtypeStruct(sc.shape, F32), jax.ShapeDtypeStruct(sp.shape, F32)),
        grid=(SAMPLE_STEPS,),
        in_specs=[v_s, u_s, conv_rows, conv_next, pool_rows, pool_next, x_s, *w_s],
        out_specs=(pl.BlockSpec((nb, D_MODEL), lambda c: (0, 0)), conv_rows, pool_rows),
        scratch_shapes=[
            pltpu.VMEM((nb, CONV_DIM), F32),
            pltpu.VMEM((nb, POOL_DIM), F32),
            pltpu.VMEM((nb, D_MODEL), BF16),
            pltpu.VMEM((nb, POOL_DIM), BF16),
        ],
    )


def _ple_rows(x, p_ref, gp_ref, wg_ref, wp_ref, gf_ref, o_ref):
    r = _rms(x, gp_ref[...]).astype(BF16)
    p = p_ref[...].astype(BF16)
    nc = 512
    for c in range(D_MODEL // nc):
        sl = slice(c * nc, (c + 1) * nc)
        gate = jax.nn.sigmoid(jnp.dot(r, wg_ref[:, sl], preferred_element_type=F32))
        proj = jnp.dot(p, wp_ref[:, sl], preferred_element_type=F32)
        o_ref[:, sl] = x[:, sl] + gate * proj
    o_ref[...] = _rms(o_ref[...], gf_ref[...])


def _ple_kernel(xh_ref, xr_ref, p_ref, xs_ref, psm_ref, gp_ref, wg_ref, wp_ref, gf_ref, o_ref, os_ref):
    _ple_rows(_row_tile(pl.program_id(0), xh_ref, xr_ref), p_ref, gp_ref, wg_ref, wp_ref, gf_ref, o_ref)

    @pl.when(pl.program_id(0) == pl.num_programs(0) - 1)
    def _():
        _ple_rows(xs_ref[...], psm_ref, gp_ref, wg_ref, wp_ref, gf_ref, os_ref)


def _ple(x, p, xs, ps, gp, wg, wp, gf, *, tm):
    rows, ns = p.shape[0], xs.shape[0]
    return _call(
        _ple_kernel, *x, p, xs, ps, gp, wg, wp, gf,
        name="ple", semantics=("arbitrary",),
        out_shape=(jax.ShapeDtypeStruct((rows, D_MODEL), F32), jax.ShapeDtypeStruct((ns, D_MODEL), F32)),
        grid=(rows // tm,),
        in_specs=[
            *_row_tile_specs(x, tm, lambda i: i),
            pl.BlockSpec((tm, PLE_DIM), lambda i: (i, 0)),
            _resident(xs.shape), _resident(ps.shape),
            _resident((1, D_MODEL)), _resident(wg.shape), _resident(wp.shape), _resident((1, D_MODEL)),
        ],
        out_specs=(pl.BlockSpec((tm, D_MODEL), lambda i: (i, 0)),
                   pl.BlockSpec((ns, D_MODEL), lambda i: (0, 0))),
    )


def kernel(x_prompt, x_sample, state_conv, state_pool, p_prompt, p_sample, norm_ffn1, w_ffn1_in, w_ffn1_out, norm_mix, w_in, conv_w, conv_b, conv_ln_g, conv_ln_b, pool_w, pool_scale, w_out, norm_ffn2, w_ffn2_in, w_ffn2_out, norm_ple, w_ple_gate, w_ple_proj, norm_final):
    assert norm_ffn1.shape[0] == 1, "the final norm is fused into the layer's last stage: one layer only"
    batch, seq, _ = x_prompt.shape
    nb = x_sample.shape[0]
    xp = x_prompt.reshape(batch * seq, D_MODEL)
    xs = x_sample.reshape(nb, D_MODEL)
    row = lambda a: a.reshape(1, -1)

    g1, gm, g2, gp, gf = row(norm_ffn1[0]), row(norm_mix[0]), row(norm_ffn2[0]), row(norm_ple[0]), row(norm_final)
    pool_w2d = pool_w[0].reshape(len(POOL_WINDOWS) * POOL_GROUP_DIM, POOL_GROUP_DIM)

    x1, x1s, (wi, wo, pw) = _ffn(xp, xs, g1, w_ffn1_in[0], w_ffn1_out[0], tm=FFN_TM, tf=FFN_TF,
                                 side=(w_in[0], w_out[0], pool_w2d))
    pw = pw.reshape(pool_w[0].shape)
    mix = (conv_w[0], row(conv_b[0]), row(conv_ln_g[0]), row(conv_ln_b[0]), pw, row(pool_scale[0]), wo)
    a, v_tail, u_tail, vs, us = _mix_prompt(x1, x1s, gm, wi, *mix[:-1], batch=batch, seq=seq, tm=PROMPT_TM)
    x2 = _out_proj(x1, a, wo, tm=PROMPT_TM)
    to_ctx_major = lambda st: jnp.transpose(st, (1, 0, 2))
    x2s, conv_t, pool_t = _mix_out_sample(vs, us, to_ctx_major(state_conv[0]), to_ctx_major(state_pool[0]), x1s, *mix)
    new_conv_s, new_pool_s = jnp.transpose(conv_t, (1, 0, 2)), jnp.transpose(pool_t, (1, 0, 2))
    x3, x3s, (wg, wp) = _ffn(x2, x2s, g2, w_ffn2_in[0], w_ffn2_out[0], tm=FFN_TM, tf=FFN_TF,
                             side=(w_ple_gate[0], w_ple_proj[0]))
    yp, ys = _ple(x3, p_prompt[0].reshape(batch * seq, PLE_DIM), x3s, p_sample[0].reshape(nb, PLE_DIM),
                  gp, wg, wp, gf, tm=PROMPT_TM)
    new_conv_p = v_tail[:, CONV_HALO - CONV_CTX:]
    new_pool_p = u_tail[:, POOL_HALO - POOL_CTX:]

    return (yp.reshape(batch, seq, D_MODEL), ys.reshape(nb, 1, D_MODEL),
            new_conv_p[None], new_conv_s[None], new_pool_p[None], new_pool_s[None])
```

```python
import functools
import math

import jax
import jax.numpy as jnp
from jax import lax
from jax.experimental import pallas as pl
from jax.experimental.pallas import tpu as pltpu

D_MODEL = 2048
D_FF = 5632
CONV_DIM = 1024
POOL_DIM = 1024
POOL_WINDOWS = (2, 4, 8, 16)
POOL_GROUP_DIM = POOL_DIM // len(POOL_WINDOWS)
POOL_CTX = max(POOL_WINDOWS) - 1
CONV_WIDTH = 31
CONV_CTX = CONV_WIDTH - 1
PLE_DIM = 256
PAST_LEN = 16384
EPS = 1e-6

F32 = jnp.float32
BF16 = jnp.bfloat16

V7X_VMEM_BYTES = 64 * 1024 * 1024
VMEM_HEADROOM_BYTES = 2 * 1024 * 1024
VMEM_TEMP_BYTES = 16 * 1024 * 1024
PROMPT_TM = 512
FFN_TM = 1024
FFN_TF = 512
HEAD_RING = 3
BODY_RING = 2
CONV_HALO = 32
POOL_HALO = 16
ROW_CHUNK = 64
LANE_GROUP = 256
SAMPLE_STEPS = 5
CONV_STEP_ROWS = CONV_CTX // SAMPLE_STEPS
POOL_STEP_ROWS = POOL_CTX // SAMPLE_STEPS
SIDE_BLOCK_ROWS = 32
SUBLANES = 8


def _rms(x, g):
    ms = jnp.mean(x * x, axis=-1, keepdims=True)
    return x * lax.rsqrt(ms + EPS) * g


def _window_bytes(spec, dtype):
    if spec.block_shape is None:
        return 0
    buffers = 1 if spec.pipeline_mode is not None else 2
    return math.prod(spec.block_shape) * jnp.dtype(dtype).itemsize * buffers


def _call(kernel, *args, name, grid, in_specs, out_specs, out_shape, semantics, scratch_shapes=(), **kwargs):
    outs, ospecs = (out_shape, out_specs) if isinstance(out_shape, tuple) else ((out_shape,), (out_specs,))
    need = (sum(_window_bytes(sp, a.dtype) for sp, a in zip(in_specs, args))
            + sum(_window_bytes(sp, o.dtype) for sp, o in zip(ospecs, outs))
            + sum(math.prod(sc.shape) * jnp.dtype(sc.dtype).itemsize
                  for sc in scratch_shapes if sc.memory_space == pltpu.VMEM))
    limit = min(need + VMEM_TEMP_BYTES, V7X_VMEM_BYTES - VMEM_HEADROOM_BYTES)
    call = pl.pallas_call(
        kernel, out_shape=out_shape, grid=grid, in_specs=in_specs, out_specs=out_specs,
        scratch_shapes=list(scratch_shapes), name=name,
        compiler_params=pltpu.CompilerParams(dimension_semantics=semantics, vmem_limit_bytes=limit), **kwargs)
    return call(*args)


def _resident(shape):
    nd = len(shape)
    return pl.BlockSpec(shape, lambda *_: (0,) * nd, pipeline_mode=pl.Buffered(1))


def _ffn_rows(j, x_ref, g_ref, w_refs, o_ref, h_ref, bf16_copy_refs=(None, None, None)):
    @pl.when(j == 0)
    def _():
        x = x_ref[...]
        h_ref[...] = _rms(x, g_ref[...]).astype(BF16)
        o_ref[...] = x

    def weight(k):
        w = w_refs[k][...].astype(BF16)
        if bf16_copy_refs[k] is not None:
            bf16_copy_refs[k][...] = w
        return w

    h = h_ref[...]
    gate = jnp.dot(h, weight(0), preferred_element_type=F32)
    up = jnp.dot(h, weight(1), preferred_element_type=F32)
    act = (gate * jax.nn.sigmoid(gate) * up * 0.5).astype(BF16)
    o_ref[...] += jnp.dot(act, weight(2), preferred_element_type=F32)


def _ffn_head_kernel(x_ref, g_ref, w_in_hbm, w_out_hbm, o_ref, wgb_ref, wub_ref, wob_ref,
                     h_ref, wg_buf, wu_buf, wo_buf, sem, *, hf):
    j = pl.program_id(0)
    nh = pl.num_programs(0)

    def block_copies(step):
        slot = lax.rem(step, HEAD_RING)
        c0 = pl.multiple_of(step * hf, hf)
        return (pltpu.make_async_copy(w_in_hbm.at[:, pl.ds(c0, hf)], wg_buf.at[slot], sem.at[0, slot]),
                pltpu.make_async_copy(w_in_hbm.at[:, pl.ds(D_FF + c0, hf)], wu_buf.at[slot], sem.at[1, slot]),
                pltpu.make_async_copy(w_out_hbm.at[pl.ds(c0, hf), :], wo_buf.at[slot], sem.at[2, slot]))

    @pl.when(j == 0)
    def _():
        for step in range(HEAD_RING - 1):
            for c in block_copies(step):
                c.start()

    @pl.when(j + HEAD_RING - 1 < nh)
    def _():
        for c in block_copies(j + HEAD_RING - 1):
            c.start()

    for c in block_copies(j):
        c.wait()
    slot = lax.rem(j, HEAD_RING)
    _ffn_rows(j, x_ref, g_ref, (wg_buf.at[slot], wu_buf.at[slot], wo_buf.at[slot]), o_ref, h_ref,
              bf16_copy_refs=(wgb_ref, wub_ref, wob_ref))


def _ffn_body_kernel(*refs, tm, tf, n_side):
    xp_ref, xs_ref, g_ref, wg_hbm, wu_hbm, wo_hbm = refs[:6]
    side_in = refs[6:6 + n_side]
    op_ref, os_ref = refs[6 + n_side:8 + n_side]
    side_out = refs[8 + n_side:8 + 2 * n_side]
    h_ref, wg_buf, wu_buf, wo_buf, sem = refs[8 + 2 * n_side:]
    i, j = pl.program_id(0), pl.program_id(1)
    nf = pl.num_programs(1)
    step = i * nf + j
    n_steps = pl.num_programs(0) * nf
    ns = xs_ref.shape[0]

    def block_copies(t):
        slot = lax.rem(t, BODY_RING)
        c0 = pl.multiple_of(lax.rem(t, nf) * tf, tf)
        return (pltpu.make_async_copy(wg_hbm.at[:, pl.ds(c0, tf)], wg_buf.at[slot], sem.at[0, slot]),
                pltpu.make_async_copy(wu_hbm.at[:, pl.ds(c0, tf)], wu_buf.at[slot], sem.at[1, slot]),
                pltpu.make_async_copy(wo_hbm.at[pl.ds(c0, tf), :], wo_buf.at[slot], sem.at[2, slot]))

    @pl.when(step == 0)
    def _():
        for t in range(BODY_RING - 1):
            for c in block_copies(t):
                c.start()

    @pl.when(step + BODY_RING - 1 < n_steps)
    def _():
        for c in block_copies(step + BODY_RING - 1):
            c.start()

    for c in block_copies(step):
        c.wait()
    for src, dst in zip(side_in, side_out):
        dst[...] = src[...].astype(BF16)
    slot = lax.rem(step, BODY_RING)
    w_refs = (wg_buf.at[slot], wu_buf.at[slot], wo_buf.at[slot])
    _ffn_rows(j, xp_ref, g_ref, w_refs, op_ref, h_ref.at[0:tm])

    @pl.when(i == pl.num_programs(0) - 1)
    def _():
        _ffn_rows(j, xs_ref, g_ref, w_refs, os_ref, h_ref.at[tm:tm + ns])


def _ffn(xp, xs, g, w_in, w_out, *, tm, tf, side=()):
    ns = xs.shape[0]
    nf = D_FF // tf
    hf = tf // 2
    nh = D_FF // hf
    once = dict(pipeline_mode=pl.Buffered(1))
    assert nh >= HEAD_RING
    head, wg, wu, wo = _call(
        functools.partial(_ffn_head_kernel, hf=hf), xp, g, w_in, w_out,
        name="ffn_head", semantics=("arbitrary",),
        out_shape=(jax.ShapeDtypeStruct((tm, D_MODEL), F32),
                   jax.ShapeDtypeStruct((D_MODEL, D_FF), BF16),
                   jax.ShapeDtypeStruct((D_MODEL, D_FF), BF16),
                   jax.ShapeDtypeStruct((D_FF, D_MODEL), BF16)),
        grid=(nh,),
        in_specs=[
            pl.BlockSpec((tm, D_MODEL), lambda j: (0, 0), **once),
            pl.BlockSpec((1, D_MODEL), lambda j: (0, 0), **once),
            pl.BlockSpec(memory_space=pl.ANY),
            pl.BlockSpec(memory_space=pl.ANY),
        ],
        out_specs=(pl.BlockSpec((tm, D_MODEL), lambda j: (0, 0)),
                   pl.BlockSpec((D_MODEL, hf), lambda j: (0, j)),
                   pl.BlockSpec((D_MODEL, hf), lambda j: (0, j)),
                   pl.BlockSpec((hf, D_MODEL), lambda j: (j, 0))),
        scratch_shapes=[pltpu.VMEM((tm, D_MODEL), BF16),
                        pltpu.VMEM((HEAD_RING, D_MODEL, hf), F32),
                        pltpu.VMEM((HEAD_RING, D_MODEL, hf), F32),
                        pltpu.VMEM((HEAD_RING, hf, D_MODEL), F32),
                        pltpu.SemaphoreType.DMA((3, HEAD_RING))],
    )

    n_body_tiles = xp.shape[0] // tm - 1
    side_rows = [max(SIDE_BLOCK_ROWS, -(-a.shape[0] // (n_body_tiles * nf))) for a in side]
    assert all(a.shape[0] % r == 0 for a, r in zip(side, side_rows))

    def side_spec(a, r):
        last = a.shape[0] // r - 1
        return pl.BlockSpec((r, a.shape[1]), lambda i, j: (jnp.minimum(i * nf + j, last), 0))

    side_specs = [side_spec(a, r) for a, r in zip(side, side_rows)]
    rest, out_s, *side_bf16 = _call(
        functools.partial(_ffn_body_kernel, tm=tm, tf=tf, n_side=len(side)), xp, xs, g, wg, wu, wo, *side,
        name="ffn_body", semantics=("arbitrary", "arbitrary"),
        out_shape=(jax.ShapeDtypeStruct((xp.shape[0] - tm, D_MODEL), F32), jax.ShapeDtypeStruct(xs.shape, F32),
                   *[jax.ShapeDtypeStruct(a.shape, BF16) for a in side]),
        grid=(n_body_tiles, nf),
        in_specs=[
            pl.BlockSpec((tm, D_MODEL), lambda i, j: (i + 1, 0)),
            pl.BlockSpec((ns, D_MODEL), lambda i, j: (0, 0), **once),
            pl.BlockSpec((1, D_MODEL), lambda i, j: (0, 0), **once),
            pl.BlockSpec(memory_space=pl.ANY),
            pl.BlockSpec(memory_space=pl.ANY),
            pl.BlockSpec(memory_space=pl.ANY),
            *side_specs,
        ],
        out_specs=(pl.BlockSpec((tm, D_MODEL), lambda i, j: (i, 0)),
                   pl.BlockSpec((ns, D_MODEL), lambda i, j: (0, 0)),
                   *side_specs),
        scratch_shapes=[pltpu.VMEM((tm + ns, D_MODEL), BF16),
                        pltpu.VMEM((BODY_RING, D_MODEL, tf), BF16),
                        pltpu.VMEM((BODY_RING, D_MODEL, tf), BF16),
                        pltpu.VMEM((BODY_RING, tf, D_MODEL), BF16),
                        pltpu.SemaphoreType.DMA((3, BODY_RING))],
    )
    return (head, rest), out_s, side_bf16


def _row_tile_specs(x, tm, tile_of):
    n_head = x[0].shape[0] // tm
    return (pl.BlockSpec((tm, D_MODEL), lambda *idx: (jnp.minimum(tile_of(*idx), n_head - 1), 0)),
            pl.BlockSpec((tm, D_MODEL), lambda *idx: (jnp.maximum(tile_of(*idx) - n_head, 0), 0)))


def _row_tile(tile, head_ref, rest_ref):
    n_head = FFN_TM // head_ref.shape[0]
    return jnp.where(tile < n_head, head_ref[...], rest_ref[...])


def _project(x, g_ref, w_ref, put_v, put_u):
    h = _rms(x, g_ref[...]).astype(BF16)
    nc = 512
    for c in range(CONV_DIM // nc):
        sl = slice(c * nc, (c + 1) * nc)
        a_val = jnp.dot(h, w_ref[:, sl], preferred_element_type=F32)
        a_gate = jnp.dot(h, w_ref[:, CONV_DIM + c * nc:CONV_DIM + (c + 1) * nc], preferred_element_type=F32)
        put_v(sl, a_val * jax.nn.sigmoid(a_gate))
        put_u(sl, jnp.dot(h, w_ref[:, 2 * CONV_DIM + c * nc:2 * CONV_DIM + (c + 1) * nc],
                          preferred_element_type=F32))


def _conv_post(acc, cb, lg, lb):
    y = acc + cb
    mu = jnp.mean(y, axis=-1, keepdims=True)
    yc = y - mu
    var = jnp.mean(yc * yc, axis=-1, keepdims=True)
    z = yc * lax.rsqrt(var + EPS) * lg + lb
    return z * jax.nn.sigmoid(z)


def _mix_tail(a_ref, d_ref, x_ref, pw_ref, ps_ref, wo_ref, o_ref):
    for g in range(len(POOL_WINDOWS)):
        sl = slice(g * POOL_GROUP_DIM, (g + 1) * POOL_GROUP_DIM)
        y = jnp.dot(d_ref[:, sl], pw_ref[g], preferred_element_type=F32) * ps_ref[:, sl]
        a_ref[:, CONV_DIM + g * POOL_GROUP_DIM:CONV_DIM + (g + 1) * POOL_GROUP_DIM] = y.astype(BF16)
    o_ref[...] = x_ref[...] + jnp.dot(a_ref[...], wo_ref[...], preferred_element_type=F32)


def _rows_above(tiles, b):
    rot = [pltpu.roll(t, SUBLANES - b, axis=0) for t in tiles]
    own = lax.broadcasted_iota(jnp.int32, tiles[0].shape, 0) < SUBLANES - b
    return [jnp.where(own, rot[j], rot[j + 1]) for j in range(len(tiles) - 1)]


def _rows_below(tiles, d):
    rot = [pltpu.roll(t, d, axis=0) for t in tiles]
    own = lax.broadcasted_iota(jnp.int32, tiles[0].shape, 0) >= d
    return [jnp.where(own, rot[j], rot[max(j - 1, 0)]) for j in range(len(tiles))]


def _conv_chunk(v_ref, wb_ref, r0, sl):
    n_out = ROW_CHUNK // SUBLANES
    n_src = n_out + CONV_HALO // SUBLANES
    src = [v_ref[r0 + SUBLANES * j:r0 + SUBLANES * (j + 1), sl] for j in range(n_src)]
    acc = [None] * n_out
    for b in range(SUBLANES):
        taps = [k for k in range(CONV_WIDTH) if (CONV_HALO - CONV_CTX + k) % SUBLANES == b]
        if not taps:
            continue
        shifted = src if b == 0 else _rows_above(src, b)
        for k in taps:
            a = (CONV_HALO - CONV_CTX + k) // SUBLANES
            wk = wb_ref[k, :, sl]
            for i in range(n_out):
                term = wk * shifted[a + i]
                acc[i] = term if acc[i] is None else acc[i] + term
    return jnp.concatenate(acc, axis=0)


def _pool_chunk(u_ref, r0, pos0, d_ref):
    n_out = ROW_CHUNK // SUBLANES
    n_ctx = POOL_HALO // SUBLANES
    for g, w in enumerate(POOL_WINDOWS):
        sl = slice(g * POOL_GROUP_DIM, (g + 1) * POOL_GROUP_DIM)
        cur = [u_ref[r0 + SUBLANES * j:r0 + SUBLANES * (j + 1), sl] for j in range(n_out + n_ctx)]
        s, span = cur, 1
        while span < w:
            if span < SUBLANES:
                below = _rows_below(s, span)
            else:
                below = [s[0]] + s[:-1]
            s = [x + y for x, y in zip(s, below)]
            span *= 2
        d = []
        for i in range(n_out):
            pos = pos0 + r0 + SUBLANES * i + lax.broadcasted_iota(jnp.int32, cur[0].shape, 0)
            cnt = jnp.minimum(pos + 1, w).astype(F32)
            d.append(s[n_ctx + i] / cnt - cur[n_ctx + i])
        d_ref[r0:r0 + ROW_CHUNK, sl] = jnp.concatenate(d, axis=0).astype(BF16)


def _mix_prompt_kernel(xh_ref, xr_ref, xs_ref, g_ref, w_ref, cw_ref, cb_ref, lg_ref, lb_ref, pw_ref, ps_ref,
                       a_ref, ctail_ref, ptail_ref, vs_ref, us_ref, vext_ref, uext_ref, d_ref, wb_ref,
                       *, tm, tiles_per_seq):
    s = pl.program_id(0)
    n_tiles = pl.num_programs(0) - 1
    slot = lax.rem(s, 2)
    v_new, u_new = vext_ref.at[slot], uext_ref.at[slot]
    v_old, u_old = vext_ref.at[1 - slot], uext_ref.at[1 - slot]

    @pl.when(s == 0)
    def _():
        vext_ref[1] = jnp.zeros(vext_ref.shape[1:], F32)
        uext_ref[1] = jnp.zeros(uext_ref.shape[1:], F32)
        for k in range(CONV_WIDTH):
            wb_ref[k] = jnp.broadcast_to(cw_ref[k:k + 1, :], wb_ref.shape[1:])

    def put_v(sl, val):
        v_new[CONV_HALO:, sl] = val

    def put_u(sl, val):
        u_new[POOL_HALO:, sl] = val

    tile = jnp.minimum(s, n_tiles - 1)
    _project(_row_tile(tile, xh_ref, xr_ref), g_ref, w_ref, put_v, put_u)
    starts_seq = lax.rem(tile, tiles_per_seq) == 0
    v_new[0:CONV_HALO, :] = jnp.where(starts_seq, 0.0, v_old[tm:tm + CONV_HALO, :])
    u_new[0:POOL_HALO, :] = jnp.where(starts_seq, 0.0, u_old[tm:tm + POOL_HALO, :])

    pos0 = lax.rem(jnp.maximum(s - 1, 0), tiles_per_seq) * tm
    cb, lg, lb = cb_ref[...], lg_ref[...], lb_ref[...]
    for r in range(tm // ROW_CHUNK):
        r0 = r * ROW_CHUNK
        conv = jnp.concatenate(
            [_conv_chunk(v_old, wb_ref, r0, slice(c * LANE_GROUP, (c + 1) * LANE_GROUP))
             for c in range(CONV_DIM // LANE_GROUP)], axis=1)
        a_ref[r0:r0 + ROW_CHUNK, 0:CONV_DIM] = _conv_post(conv, cb, lg, lb).astype(BF16)
        _pool_chunk(u_old, r0, pos0, d_ref)
    for g in range(len(POOL_WINDOWS)):
        sl = slice(g * POOL_GROUP_DIM, (g + 1) * POOL_GROUP_DIM)
        y = jnp.dot(d_ref[:, sl], pw_ref[g], preferred_element_type=F32) * ps_ref[:, sl]
        a_ref[:, CONV_DIM + g * POOL_GROUP_DIM:CONV_DIM + (g + 1) * POOL_GROUP_DIM] = y.astype(BF16)
    ctail_ref[0] = v_old[tm:tm + CONV_HALO, :]
    ptail_ref[0] = u_old[tm:tm + POOL_HALO, :]

    @pl.when(s == n_tiles)
    def _():
        def put_vs(sl, val):
            vs_ref[:, sl] = val

        def put_us(sl, val):
            us_ref[:, sl] = val

        _project(xs_ref[...], g_ref, w_ref, put_vs, put_us)


def _mix_prompt(x, xs, g, w_in, cw, cb, lg, lb, pw, ps, *, batch, seq, tm):
    tiles_per_seq = seq // tm
    n_tiles = batch * tiles_per_seq
    ns = xs.shape[0]
    prev = lambda s: (jnp.maximum(s - 1, 0), 0)
    prev_seq = lambda s: (jnp.maximum(s - 1, 0) // tiles_per_seq, 0, 0)
    return _call(
        functools.partial(_mix_prompt_kernel, tm=tm, tiles_per_seq=tiles_per_seq),
        *x, xs, g, w_in, cw, cb, lg, lb, pw, ps,
        name="mix_prompt", semantics=("arbitrary",),
        out_shape=(jax.ShapeDtypeStruct((batch * seq, D_MODEL), BF16),
                   jax.ShapeDtypeStruct((batch, CONV_HALO, CONV_DIM), F32),
                   jax.ShapeDtypeStruct((batch, POOL_HALO, POOL_DIM), F32),
                   jax.ShapeDtypeStruct((ns, CONV_DIM), F32),
                   jax.ShapeDtypeStruct((ns, POOL_DIM), F32)),
        grid=(n_tiles + 1,),
        in_specs=[
            *_row_tile_specs(x, tm, lambda s: jnp.minimum(s, n_tiles - 1)), _resident(xs.shape),
            _resident(g.shape), _resident(w_in.shape), _resident(cw.shape), _resident(cb.shape),
            _resident(lg.shape), _resident(lb.shape), _resident(pw.shape), _resident(ps.shape),
        ],
        out_specs=(pl.BlockSpec((tm, D_MODEL), prev),
                   pl.BlockSpec((1, CONV_HALO, CONV_DIM), prev_seq),
                   pl.BlockSpec((1, POOL_HALO, POOL_DIM), prev_seq),
                   pl.BlockSpec((ns, CONV_DIM), lambda s: (0, 0)),
                   pl.BlockSpec((ns, POOL_DIM), lambda s: (0, 0))),
        scratch_shapes=[
            pltpu.VMEM((2, CONV_HALO + tm, CONV_DIM), F32),
            pltpu.VMEM((2, POOL_HALO + tm, POOL_DIM), F32),
            pltpu.VMEM((tm, POOL_DIM), BF16),
            pltpu.VMEM((CONV_WIDTH, SUBLANES, CONV_DIM), F32),
        ],
    )


def _out_proj_kernel(xh_ref, xr_ref, a_ref, wo_ref, o_ref):
    x = _row_tile(pl.program_id(0), xh_ref, xr_ref)
    o_ref[...] = x + jnp.dot(a_ref[...], wo_ref[...], preferred_element_type=F32)


def _out_proj(x, a, wo, *, tm):
    rows = a.shape[0]
    return _call(
        _out_proj_kernel, *x, a, wo,
        name="out_proj", semantics=("parallel",),
        out_shape=jax.ShapeDtypeStruct((rows, D_MODEL), F32),
        grid=(rows // tm,),
        in_specs=[*_row_tile_specs(x, tm, lambda i: i),
                  pl.BlockSpec((tm, D_MODEL), lambda i: (i, 0)),
                  _resident(wo.shape)],
        out_specs=pl.BlockSpec((tm, D_MODEL), lambda i: (i, 0)),
    )


def _mix_out_sample_kernel(v_ref, u_ref, sc_ref, sc_next_ref, sp_ref, sp_next_ref, x_ref, cw_ref, cb_ref, lg_ref, lb_ref,
                           pw_ref, ps_ref, wo_ref, o_ref, nsc_ref, nsp_ref, acc_ref, s_ref, a_ref, d_ref):
    c = pl.program_id(0)
    last = c == pl.num_programs(0) - 1

    @pl.when(c == 0)
    def _():
        acc_ref[...] = jnp.zeros(acc_ref.shape, F32)
        s_ref[...] = jnp.zeros(s_ref.shape, F32)

    acc = acc_ref[...]
    for r in range(CONV_STEP_ROWS):
        acc = acc + cw_ref[pl.ds(c * CONV_STEP_ROWS + r, 1), :] * sc_ref[r]
    acc_ref[...] = acc
    for g, w in enumerate(POOL_WINDOWS):
        sl = slice(g * POOL_GROUP_DIM, (g + 1) * POOL_GROUP_DIM)
        s = s_ref[:, sl]
        for r in range(POOL_STEP_ROWS):
            s = s + jnp.where(c * POOL_STEP_ROWS + r >= POOL_CTX + 1 - w, sp_ref[r, :, sl], 0.0)
        s_ref[:, sl] = s

    for r in range(CONV_STEP_ROWS - 1):
        nsc_ref[r] = sc_ref[r + 1]
    nsc_ref[CONV_STEP_ROWS - 1] = jnp.where(last, v_ref[...], sc_next_ref[0])
    for r in range(POOL_STEP_ROWS - 1):
        nsp_ref[r] = sp_ref[r + 1]
    nsp_ref[POOL_STEP_ROWS - 1] = jnp.where(last, u_ref[...], sp_next_ref[0])

    @pl.when(last)
    def _():
        conv = acc_ref[...] + cw_ref[CONV_CTX:CONV_WIDTH, :] * v_ref[...]
        a_ref[:, 0:CONV_DIM] = _conv_post(conv, cb_ref[...], lg_ref[...], lb_ref[...]).astype(BF16)
        u = u_ref[...]
        s = s_ref[...] + u
        lane1 = lax.broadcasted_iota(jnp.int32, (1, POOL_DIM), 1)
        cnt = jnp.zeros((1, POOL_DIM), F32)
        for g, w in enumerate(POOL_WINDOWS):
            cnt = jnp.where(lane1 // POOL_GROUP_DIM == g, float(min(PAST_LEN + 1, w)), cnt)
        d_ref[...] = (s / cnt - u).astype(BF16)
        _mix_tail(a_ref, d_ref, x_ref, pw_ref, ps_ref, wo_ref, o_ref)


def _mix_out_sample(v, u, sc, sp, x, cw, cb, lg, lb, pw, ps, wo):
    nb = v.shape[0]
    whole = (v, u, x, cw, cb, lg, lb, pw, ps, wo)
    v_s, u_s, x_s, *w_s = [_resident(a.shape) for a in whole]
    rows = lambda n, index: pl.BlockSpec((n, nb, sc.shape[2]), lambda c: (index(c), 0, 0))
    conv_rows = rows(CONV_STEP_ROWS, lambda c: c)
    conv_next = rows(1, lambda c: jnp.minimum((c + 1) * CONV_STEP_ROWS, CONV_CTX - 1))
    pool_rows = rows(POOL_STEP_ROWS, lambda c: c)
    pool_next = rows(1, lambda c: jnp.minimum((c + 1) * POOL_STEP_ROWS, POOL_CTX - 1))
    return _call(
        _mix_out_sample_kernel, v, u, sc, sc, sp, sp, x, cw, cb, lg, lb, pw, ps, wo,
        name="mix_out_sample", semantics=("arbitrary",),
        out_shape=(jax.ShapeDtypeStruct((nb, D_MODEL), F32),
                   jax.ShapeDtypeStruct(sc.shape, F32), jax.ShapeDtypeStruct(sp.shape, F32)),
        grid=(SAMPLE_STEPS,),
        in_specs=[v_s, u_s, conv_rows, conv_next, pool_rows, pool_next, x_s, *w_s],
        out_specs=(pl.BlockSpec((nb, D_MODEL), lambda c: (0, 0)), conv_rows, pool_rows),
        scratch_shapes=[
            pltpu.VMEM((nb, CONV_DIM), F32),
            pltpu.VMEM((nb, POOL_DIM), F32),
            pltpu.VMEM((nb, D_MODEL), BF16),
            pltpu.VMEM((nb, POOL_DIM), BF16),
        ],
    )


def _ple_rows(x, p_ref, gp_ref, wg_ref, wp_ref, gf_ref, o_ref):
    r = _rms(x, gp_ref[...]).astype(BF16)
    p = p_ref[...].astype(BF16)
    nc = 512
    for c in range(D_MODEL // nc):
        sl = slice(c * nc, (c + 1) * nc)
        gate = jax.nn.sigmoid(jnp.dot(r, wg_ref[:, sl], preferred_element_type=F32))
        proj = jnp.dot(p, wp_ref[:, sl], preferred_element_type=F32)
        o_ref[:, sl] = x[:, sl] + gate * proj
    o_ref[...] = _rms(o_ref[...], gf_ref[...])


def _ple_kernel(xh_ref, xr_ref, p_ref, xs_ref, psm_ref, gp_ref, wg_ref, wp_ref, gf_ref, o_ref, os_ref):
    _ple_rows(_row_tile(pl.program_id(0), xh_ref, xr_ref), p_ref, gp_ref, wg_ref, wp_ref, gf_ref, o_ref)

    @pl.when(pl.program_id(0) == pl.num_programs(0) - 1)
    def _():
        _ple_rows(xs_ref[...], psm_ref, gp_ref, wg_ref, wp_ref, gf_ref, os_ref)


def _ple(x, p, xs, ps, gp, wg, wp, gf, *, tm):
    rows, ns = p.shape[0], xs.shape[0]
    return _call(
        _ple_kernel, *x, p, xs, ps, gp, wg, wp, gf,
        name="ple", semantics=("arbitrary",),
        out_shape=(jax.ShapeDtypeStruct((rows, D_MODEL), F32), jax.ShapeDtypeStruct((ns, D_MODEL), F32)),
        grid=(rows // tm,),
        in_specs=[
            *_row_tile_specs(x, tm, lambda i: i),
            pl.BlockSpec((tm, PLE_DIM), lambda i: (i, 0)),
            _resident(xs.shape), _resident(ps.shape),
            _resident((1, D_MODEL)), _resident(wg.shape), _resident(wp.shape), _resident((1, D_MODEL)),
        ],
        out_specs=(pl.BlockSpec((tm, D_MODEL), lambda i: (i, 0)),
                   pl.BlockSpec((ns, D_MODEL), lambda i: (0, 0))),
    )


def kernel(x_prompt, x_sample, state_conv, state_pool, p_prompt, p_sample, norm_ffn1, w_ffn1_in, w_ffn1_out, norm_mix, w_in, conv_w, conv_b, conv_ln_g, conv_ln_b, pool_w, pool_scale, w_out, norm_ffn2, w_ffn2_in, w_ffn2_out, norm_ple, w_ple_gate, w_ple_proj, norm_final):
    assert norm_ffn1.shape[0] == 1, "the final norm is fused into the layer's last stage: one layer only"
    batch, seq, _ = x_prompt.shape
    nb = x_sample.shape[0]
    xp = x_prompt.reshape(batch * seq, D_MODEL)
    xs = x_sample.reshape(nb, D_MODEL)
    row = lambda a: a.reshape(1, -1)

    g1, gm, g2, gp, gf = row(norm_ffn1[0]), row(norm_mix[0]), row(norm_ffn2[0]), row(norm_ple[0]), row(norm_final)
    pool_w2d = pool_w[0].reshape(len(POOL_WINDOWS) * POOL_GROUP_DIM, POOL_GROUP_DIM)

    x1, x1s, (wi, wo, pw) = _ffn(xp, xs, g1, w_ffn1_in[0], w_ffn1_out[0], tm=FFN_TM, tf=FFN_TF,
                                 side=(w_in[0], w_out[0], pool_w2d))
    pw = pw.reshape(pool_w[0].shape)
    mix = (conv_w[0], row(conv_b[0]), row(conv_ln_g[0]), row(conv_ln_b[0]), pw, row(pool_scale[0]), wo)
    a, v_tail, u_tail, vs, us = _mix_prompt(x1, x1s, gm, wi, *mix[:-1], batch=batch, seq=seq, tm=PROMPT_TM)
    x2 = _out_proj(x1, a, wo, tm=PROMPT_TM)
    to_ctx_major = lambda st: jnp.transpose(st, (1, 0, 2))
    x2s, conv_t, pool_t = _mix_out_sample(vs, us, to_ctx_major(state_conv[0]), to_ctx_major(state_pool[0]), x1s, *mix)
    new_conv_s, new_pool_s = jnp.transpose(conv_t, (1, 0, 2)), jnp.transpose(pool_t, (1, 0, 2))
    x3, x3s, (wg, wp) = _ffn(x2, x2s, g2, w_ffn2_in[0], w_ffn2_out[0], tm=FFN_TM, tf=FFN_TF,
                             side=(w_ple_gate[0], w_ple_proj[0]))
    yp, ys = _ple(x3, p_prompt[0].reshape(batch * seq, PLE_DIM), x3s, p_sample[0].reshape(nb, PLE_DIM),
                  gp, wg, wp, gf, tm=PROMPT_TM)
    new_conv_p = v_tail[:, CONV_HALO - CONV_CTX:]
    new_pool_p = u_tail[:, POOL_HALO - POOL_CTX:]

    return (yp.reshape(batch, seq, D_MODEL), ys.reshape(nb, 1, D_MODEL),
            new_conv_p[None], new_conv_s[None], new_pool_p[None], new_pool_s[None])
```

```python
import functools
import math

import jax
import jax.numpy as jnp
from jax import lax
from jax.experimental import pallas as pl
from jax.experimental.pallas import tpu as pltpu

D_MODEL = 2048
D_FF = 5632
CONV_DIM = 1024
POOL_DIM = 1024
POOL_WINDOWS = (2, 4, 8, 16)
POOL_GROUP_DIM = POOL_DIM // len(POOL_WINDOWS)
POOL_CTX = max(POOL_WINDOWS) - 1
CONV_WIDTH = 31
CONV_CTX = CONV_WIDTH - 1
PLE_DIM = 256
PAST_LEN = 16384
EPS = 1e-6

F32 = jnp.float32
BF16 = jnp.bfloat16

V7X_VMEM_BYTES = 64 * 1024 * 1024
VMEM_HEADROOM_BYTES = 2 * 1024 * 1024
VMEM_TEMP_BYTES = 16 * 1024 * 1024
PROMPT_TM = 512
FFN_TM = 1024
FFN_TF = 512
HEAD_RING = 3
PLE_PARTS = 2
CONV_HALO = 32
POOL_HALO = 16
ROW_CHUNK = 64
LANE_GROUP = 256
SAMPLE_STEPS = 5
CONV_STEP_ROWS = CONV_CTX // SAMPLE_STEPS
POOL_STEP_ROWS = POOL_CTX // SAMPLE_STEPS
SIDE_BLOCK_ROWS = 32
SUBLANES = 8


def _rms(x, g):
    ms = jnp.mean(x * x, axis=-1, keepdims=True)
    return x * lax.rsqrt(ms + EPS) * g


def _window_bytes(spec, dtype):
    if spec.block_shape is None:
        return 0
    buffers = 1 if spec.pipeline_mode is not None else 2
    return math.prod(spec.block_shape) * jnp.dtype(dtype).itemsize * buffers


def _call(kernel, *args, name, grid, in_specs, out_specs, out_shape, semantics, scratch_shapes=(), **kwargs):
    outs, ospecs = (out_shape, out_specs) if isinstance(out_shape, tuple) else ((out_shape,), (out_specs,))
    need = (sum(_window_bytes(sp, a.dtype) for sp, a in zip(in_specs, args))
            + sum(_window_bytes(sp, o.dtype) for sp, o in zip(ospecs, outs))
            + sum(math.prod(sc.shape) * jnp.dtype(sc.dtype).itemsize
                  for sc in scratch_shapes if sc.memory_space == pltpu.VMEM))
    limit = min(need + VMEM_TEMP_BYTES, V7X_VMEM_BYTES - VMEM_HEADROOM_BYTES)
    call = pl.pallas_call(
        kernel, out_shape=out_shape, grid=grid, in_specs=in_specs, out_specs=out_specs,
        scratch_shapes=list(scratch_shapes), name=name,
        compiler_params=pltpu.CompilerParams(dimension_semantics=semantics, vmem_limit_bytes=limit), **kwargs)
    return call(*args)


def _resident(shape):
    nd = len(shape)
    return pl.BlockSpec(shape, lambda *_: (0,) * nd, pipeline_mode=pl.Buffered(1))


def _ffn_rows(j, x_ref, g_ref, w_refs, o_ref, h_ref, bf16_copy_refs=(None, None, None)):
    @pl.when(j == 0)
    def _():
        x = x_ref[...]
        h_ref[...] = _rms(x, g_ref[...]).astype(BF16)
        o_ref[...] = x

    def weight(k):
        w = w_refs[k][...].astype(BF16)
        if bf16_copy_refs[k] is not None:
            bf16_copy_refs[k][...] = w
        return w

    h = h_ref[...]
    gate = jnp.dot(h, weight(0), preferred_element_type=F32)
    up = jnp.dot(h, weight(1), preferred_element_type=F32)
    act = (gate * jax.nn.sigmoid(gate) * up * 0.5).astype(BF16)
    o_ref[...] += jnp.dot(act, weight(2), preferred_element_type=F32)


def _ffn_head_kernel(x_ref, g_ref, w_in_hbm, w_out_hbm, o_ref, wgb_ref, wub_ref, wob_ref,
                     h_ref, wg_buf, wu_buf, wo_buf, sem, *, hf):
    j = pl.program_id(0)
    nh = pl.num_programs(0)

    def block_copies(step):
        slot = lax.rem(step, HEAD_RING)
        c0 = pl.multiple_of(step * hf, hf)
        return (pltpu.make_async_copy(w_in_hbm.at[:, pl.ds(c0, hf)], wg_buf.at[slot], sem.at[0, slot]),
                pltpu.make_async_copy(w_in_hbm.at[:, pl.ds(D_FF + c0, hf)], wu_buf.at[slot], sem.at[1, slot]),
                pltpu.make_async_copy(w_out_hbm.at[pl.ds(c0, hf), :], wo_buf.at[slot], sem.at[2, slot]))

    @pl.when(j == 0)
    def _():
        for step in range(HEAD_RING - 1):
            for c in block_copies(step):
                c.start()

    @pl.when(j + HEAD_RING - 1 < nh)
    def _():
        for c in block_copies(j + HEAD_RING - 1):
            c.start()

    for c in block_copies(j):
        c.wait()
    slot = lax.rem(j, HEAD_RING)
    _ffn_rows(j, x_ref, g_ref, (wg_buf.at[slot], wu_buf.at[slot], wo_buf.at[slot]), o_ref, h_ref,
              bf16_copy_refs=(wgb_ref, wub_ref, wob_ref))


def _ffn_body_kernel(*refs, tm, n_side):
    xp_ref, xs_ref, g_ref, wg_ref, wu_ref, wo_ref = refs[:6]
    side_in = refs[6:6 + n_side]
    op_ref, os_ref = refs[6 + n_side:8 + n_side]
    side_out = refs[8 + n_side:8 + 2 * n_side]
    h_ref = refs[8 + 2 * n_side]
    i, j = pl.program_id(0), pl.program_id(1)
    ns = xs_ref.shape[0]
    for src, dst in zip(side_in, side_out):
        dst[...] = src[...].astype(BF16)
    w_refs = (wg_ref, wu_ref, wo_ref)
    _ffn_rows(j, xp_ref, g_ref, w_refs, op_ref, h_ref.at[0:tm])

    @pl.when(i == pl.num_programs(0) - 1)
    def _():
        _ffn_rows(j, xs_ref, g_ref, w_refs, os_ref, h_ref.at[tm:tm + ns])


def _ffn(xp, xs, g, w_in, w_out, *, tm, tf, side=()):
    ns = xs.shape[0]
    nf = D_FF // tf
    hf = tf // 2
    nh = D_FF // hf
    once = dict(pipeline_mode=pl.Buffered(1))
    assert nh >= HEAD_RING
    head, wg, wu, wo = _call(
        functools.partial(_ffn_head_kernel, hf=hf), xp, g, w_in, w_out,
        name="ffn_head", semantics=("arbitrary",),
        out_shape=(jax.ShapeDtypeStruct((tm, D_MODEL), F32),
                   jax.ShapeDtypeStruct((D_MODEL, D_FF), BF16),
                   jax.ShapeDtypeStruct((D_MODEL, D_FF), BF16),
                   jax.ShapeDtypeStruct((D_FF, D_MODEL), BF16)),
        grid=(nh,),
        in_specs=[
            pl.BlockSpec((tm, D_MODEL), lambda j: (0, 0), **once),
            pl.BlockSpec((1, D_MODEL), lambda j: (0, 0), **once),
            pl.BlockSpec(memory_space=pl.ANY),
            pl.BlockSpec(memory_space=pl.ANY),
        ],
        out_specs=(pl.BlockSpec((tm, D_MODEL), lambda j: (0, 0)),
                   pl.BlockSpec((D_MODEL, hf), lambda j: (0, j)),
                   pl.BlockSpec((D_MODEL, hf), lambda j: (0, j)),
                   pl.BlockSpec((hf, D_MODEL), lambda j: (j, 0))),
        scratch_shapes=[pltpu.VMEM((tm, D_MODEL), BF16),
                        pltpu.VMEM((HEAD_RING, D_MODEL, hf), F32),
                        pltpu.VMEM((HEAD_RING, D_MODEL, hf), F32),
                        pltpu.VMEM((HEAD_RING, hf, D_MODEL), F32),
                        pltpu.SemaphoreType.DMA((3, HEAD_RING))],
    )

    n_body_tiles = xp.shape[0] // tm - 1
    side_rows = [max(SIDE_BLOCK_ROWS, -(-a.shape[0] // (n_body_tiles * nf))) for a in side]
    assert all(a.shape[0] % r == 0 for a, r in zip(side, side_rows))

    def side_spec(a, r):
        last = a.shape[0] // r - 1
        return pl.BlockSpec((r, a.shape[1]), lambda i, j: (jnp.minimum(i * nf + j, last), 0))

    side_specs = [side_spec(a, r) for a, r in zip(side, side_rows)]
    rest, out_s, *side_bf16 = _call(
        functools.partial(_ffn_body_kernel, tm=tm, n_side=len(side)), xp, xs, g, wg, wu, wo, *side,
        name="ffn_body", semantics=("arbitrary", "arbitrary"),
        out_shape=(jax.ShapeDtypeStruct((xp.shape[0] - tm, D_MODEL), F32), jax.ShapeDtypeStruct(xs.shape, F32),
                   *[jax.ShapeDtypeStruct(a.shape, BF16) for a in side]),
        grid=(n_body_tiles, nf),
        in_specs=[
            pl.BlockSpec((tm, D_MODEL), lambda i, j: (i + 1, 0)),
            pl.BlockSpec((ns, D_MODEL), lambda i, j: (0, 0), **once),
            pl.BlockSpec((1, D_MODEL), lambda i, j: (0, 0), **once),
            pl.BlockSpec((D_MODEL, tf), lambda i, j: (0, j)),
            pl.BlockSpec((D_MODEL, tf), lambda i, j: (0, j)),
            pl.BlockSpec((tf, D_MODEL), lambda i, j: (j, 0)),
            *side_specs,
        ],
        out_specs=(pl.BlockSpec((tm, D_MODEL), lambda i, j: (i, 0)),
                   pl.BlockSpec((ns, D_MODEL), lambda i, j: (0, 0)),
                   *side_specs),
        scratch_shapes=[pltpu.VMEM((tm + ns, D_MODEL), BF16)],
    )
    return (head, rest), out_s, side_bf16


def _row_tile_specs(x, tm, tile_of):
    n_head = x[0].shape[0] // tm
    return (pl.BlockSpec((tm, D_MODEL), lambda *idx: (jnp.minimum(tile_of(*idx), n_head - 1), 0)),
            pl.BlockSpec((tm, D_MODEL), lambda *idx: (jnp.maximum(tile_of(*idx) - n_head, 0), 0)))


def _row_tile(tile, head_ref, rest_ref):
    n_head = FFN_TM // head_ref.shape[0]
    return jnp.where(tile < n_head, head_ref[...], rest_ref[...])


def _project(x, g_ref, w_ref, put_v, put_u):
    h = _rms(x, g_ref[...]).astype(BF16)
    nc = 512
    for c in range(CONV_DIM // nc):
        sl = slice(c * nc, (c + 1) * nc)
        a_val = jnp.dot(h, w_ref[:, sl], preferred_element_type=F32)
        a_gate = jnp.dot(h, w_ref[:, CONV_DIM + c * nc:CONV_DIM + (c + 1) * nc], preferred_element_type=F32)
        put_v(sl, a_val * jax.nn.sigmoid(a_gate))
        put_u(sl, jnp.dot(h, w_ref[:, 2 * CONV_DIM + c * nc:2 * CONV_DIM + (c + 1) * nc],
                          preferred_element_type=F32))


def _conv_post(acc, cb, lg, lb):
    y = acc + cb
    mu = jnp.mean(y, axis=-1, keepdims=True)
    yc = y - mu
    var = jnp.mean(yc * yc, axis=-1, keepdims=True)
    z = yc * lax.rsqrt(var + EPS) * lg + lb
    return z * jax.nn.sigmoid(z)


def _mix_tail(a_ref, d_ref, x_ref, pw_ref, ps_ref, wo_ref, o_ref):
    for g in range(len(POOL_WINDOWS)):
        sl = slice(g * POOL_GROUP_DIM, (g + 1) * POOL_GROUP_DIM)
        y = jnp.dot(d_ref[:, sl], pw_ref[g], preferred_element_type=F32) * ps_ref[:, sl]
        a_ref[:, CONV_DIM + g * POOL_GROUP_DIM:CONV_DIM + (g + 1) * POOL_GROUP_DIM] = y.astype(BF16)
    o_ref[...] = x_ref[...] + jnp.dot(a_ref[...], wo_ref[...], preferred_element_type=F32)


def _rows_above(tiles, b):
    rot = [pltpu.roll(t, SUBLANES - b, axis=0) for t in tiles]
    own = lax.broadcasted_iota(jnp.int32, tiles[0].shape, 0) < SUBLANES - b
    return [jnp.where(own, rot[j], rot[j + 1]) for j in range(len(tiles) - 1)]


def _rows_below(tiles, d):
    rot = [pltpu.roll(t, d, axis=0) for t in tiles]
    own = lax.broadcasted_iota(jnp.int32, tiles[0].shape, 0) >= d
    return [jnp.where(own, rot[j], rot[max(j - 1, 0)]) for j in range(len(tiles))]


def _conv_chunk(v_ref, wb_ref, r0, sl):
    n_out = ROW_CHUNK // SUBLANES
    n_src = n_out + CONV_HALO // SUBLANES
    src = [v_ref[r0 + SUBLANES * j:r0 + SUBLANES * (j + 1), sl] for j in range(n_src)]
    acc = [None] * n_out
    for b in range(SUBLANES):
        taps = [k for k in range(CONV_WIDTH) if (CONV_HALO - CONV_CTX + k) % SUBLANES == b]
        if not taps:
            continue
        shifted = src if b == 0 else _rows_above(src, b)
        for k in taps:
            a = (CONV_HALO - CONV_CTX + k) // SUBLANES
            wk = wb_ref[k, :, sl]
            for i in range(n_out):
                term = wk * shifted[a + i]
                acc[i] = term if acc[i] is None else acc[i] + term
    return jnp.concatenate(acc, axis=0)


def _pool_chunk(u_ref, r0, pos0, d_ref):
    n_out = ROW_CHUNK // SUBLANES
    n_ctx = POOL_HALO // SUBLANES
    for g, w in enumerate(POOL_WINDOWS):
        sl = slice(g * POOL_GROUP_DIM, (g + 1) * POOL_GROUP_DIM)
        cur = [u_ref[r0 + SUBLANES * j:r0 + SUBLANES * (j + 1), sl] for j in range(n_out + n_ctx)]
        s, span = cur, 1
        while span < w:
            if span < SUBLANES:
                below = _rows_below(s, span)
            else:
                below = [s[0]] + s[:-1]
            s = [x + y for x, y in zip(s, below)]
            span *= 2
        d = []
        for i in range(n_out):
            pos = pos0 + r0 + SUBLANES * i + lax.broadcasted_iota(jnp.int32, cur[0].shape, 0)
            cnt = jnp.minimum(pos + 1, w).astype(F32)
            d.append(s[n_ctx + i] / cnt - cur[n_ctx + i])
        d_ref[r0:r0 + ROW_CHUNK, sl] = jnp.concatenate(d, axis=0).astype(BF16)


def _mix_prompt_kernel(xh_ref, xr_ref, xs_ref, g_ref, w_ref, cw_ref, cb_ref, lg_ref, lb_ref, pw_ref, ps_ref,
                       a_ref, ctail_ref, ptail_ref, vs_ref, us_ref, vext_ref, uext_ref, d_ref, wb_ref,
                       *, tm, tiles_per_seq):
    s = pl.program_id(0)
    n_tiles = pl.num_programs(0) - 1
    slot = lax.rem(s, 2)
    v_new, u_new = vext_ref.at[slot], uext_ref.at[slot]
    v_old, u_old = vext_ref.at[1 - slot], uext_ref.at[1 - slot]

    @pl.when(s == 0)
    def _():
        vext_ref[1] = jnp.zeros(vext_ref.shape[1:], F32)
        uext_ref[1] = jnp.zeros(uext_ref.shape[1:], F32)
        for k in range(CONV_WIDTH):
            wb_ref[k] = jnp.broadcast_to(cw_ref[k:k + 1, :], wb_ref.shape[1:])

    def put_v(sl, val):
        v_new[CONV_HALO:, sl] = val

    def put_u(sl, val):
        u_new[POOL_HALO:, sl] = val

    tile = jnp.minimum(s, n_tiles - 1)
    _project(_row_tile(tile, xh_ref, xr_ref), g_ref, w_ref, put_v, put_u)
    starts_seq = lax.rem(tile, tiles_per_seq) == 0
    v_new[0:CONV_HALO, :] = jnp.where(starts_seq, 0.0, v_old[tm:tm + CONV_HALO, :])
    u_new[0:POOL_HALO, :] = jnp.where(starts_seq, 0.0, u_old[tm:tm + POOL_HALO, :])

    pos0 = lax.rem(jnp.maximum(s - 1, 0), tiles_per_seq) * tm
    cb, lg, lb = cb_ref[...], lg_ref[...], lb_ref[...]
    for r in range(tm // ROW_CHUNK):
        r0 = r * ROW_CHUNK
        conv = jnp.concatenate(
            [_conv_chunk(v_old, wb_ref, r0, slice(c * LANE_GROUP, (c + 1) * LANE_GROUP))
             for c in range(CONV_DIM // LANE_GROUP)], axis=1)
        a_ref[r0:r0 + ROW_CHUNK, 0:CONV_DIM] = _conv_post(conv, cb, lg, lb).astype(BF16)
        _pool_chunk(u_old, r0, pos0, d_ref)
    for g in range(len(POOL_WINDOWS)):
        sl = slice(g * POOL_GROUP_DIM, (g + 1) * POOL_GROUP_DIM)
        y = jnp.dot(d_ref[:, sl], pw_ref[g], preferred_element_type=F32) * ps_ref[:, sl]
        a_ref[:, CONV_DIM + g * POOL_GROUP_DIM:CONV_DIM + (g + 1) * POOL_GROUP_DIM] = y.astype(BF16)
    ctail_ref[0] = v_old[tm:tm + CONV_HALO, :]
    ptail_ref[0] = u_old[tm:tm + POOL_HALO, :]

    @pl.when(s == n_tiles)
    def _():
        def put_vs(sl, val):
            vs_ref[:, sl] = val

        def put_us(sl, val):
            us_ref[:, sl] = val

        _project(xs_ref[...], g_ref, w_ref, put_vs, put_us)


def _mix_prompt(x, xs, g, w_in, cw, cb, lg, lb, pw, ps, *, batch, seq, tm):
    tiles_per_seq = seq // tm
    n_tiles = batch * tiles_per_seq
    ns = xs.shape[0]
    prev = lambda s: (jnp.maximum(s - 1, 0), 0)
    prev_seq = lambda s: (jnp.maximum(s - 1, 0) // tiles_per_seq, 0, 0)
    return _call(
        functools.partial(_mix_prompt_kernel, tm=tm, tiles_per_seq=tiles_per_seq),
        *x, xs, g, w_in, cw, cb, lg, lb, pw, ps,
        name="mix_prompt", semantics=("arbitrary",),
        out_shape=(jax.ShapeDtypeStruct((batch * seq, D_MODEL), BF16),
                   jax.ShapeDtypeStruct((batch, CONV_HALO, CONV_DIM), F32),
                   jax.ShapeDtypeStruct((batch, POOL_HALO, POOL_DIM), F32),
                   jax.ShapeDtypeStruct((ns, CONV_DIM), F32),
                   jax.ShapeDtypeStruct((ns, POOL_DIM), F32)),
        grid=(n_tiles + 1,),
        in_specs=[
            *_row_tile_specs(x, tm, lambda s: jnp.minimum(s, n_tiles - 1)), _resident(xs.shape),
            _resident(g.shape), _resident(w_in.shape), _resident(cw.shape), _resident(cb.shape),
            _resident(lg.shape), _resident(lb.shape), _resident(pw.shape), _resident(ps.shape),
        ],
        out_specs=(pl.BlockSpec((tm, D_MODEL), prev),
                   pl.BlockSpec((1, CONV_HALO, CONV_DIM), prev_seq),
                   pl.BlockSpec((1, POOL_HALO, POOL_DIM), prev_seq),
                   pl.BlockSpec((ns, CONV_DIM), lambda s: (0, 0)),
                   pl.BlockSpec((ns, POOL_DIM), lambda s: (0, 0))),
        scratch_shapes=[
            pltpu.VMEM((2, CONV_HALO + tm, CONV_DIM), F32),
            pltpu.VMEM((2, POOL_HALO + tm, POOL_DIM), F32),
            pltpu.VMEM((tm, POOL_DIM), BF16),
            pltpu.VMEM((CONV_WIDTH, SUBLANES, CONV_DIM), F32),
        ],
    )


def _out_proj_kernel(xh_ref, xr_ref, a_ref, wo_ref, o_ref):
    x = _row_tile(pl.program_id(0), xh_ref, xr_ref)
    o_ref[...] = x + jnp.dot(a_ref[...], wo_ref[...], preferred_element_type=F32)


def _out_proj(x, a, wo, *, tm):
    rows = a.shape[0]
    return _call(
        _out_proj_kernel, *x, a, wo,
        name="out_proj", semantics=("parallel",),
        out_shape=jax.ShapeDtypeStruct((rows, D_MODEL), F32),
        grid=(rows // tm,),
        in_specs=[*_row_tile_specs(x, tm, lambda i: i),
                  pl.BlockSpec((tm, D_MODEL), lambda i: (i, 0)),
                  _resident(wo.shape)],
        out_specs=pl.BlockSpec((tm, D_MODEL), lambda i: (i, 0)),
    )


def _mix_out_sample_kernel(v_ref, u_ref, sc_ref, sc_next_ref, sp_ref, sp_next_ref, x_ref, cw_ref, cb_ref, lg_ref, lb_ref,
                           pw_ref, ps_ref, wo_ref, o_ref, nsc_ref, nsp_ref, acc_ref, s_ref, a_ref, d_ref):
    c = pl.program_id(0)
    last = c == pl.num_programs(0) - 1

    @pl.when(c == 0)
    def _():
        acc_ref[...] = jnp.zeros(acc_ref.shape, F32)
        s_ref[...] = jnp.zeros(s_ref.shape, F32)

    acc = acc_ref[...]
    for r in range(CONV_STEP_ROWS):
        acc = acc + cw_ref[pl.ds(c * CONV_STEP_ROWS + r, 1), :] * sc_ref[r]
    acc_ref[...] = acc
    for g, w in enumerate(POOL_WINDOWS):
        sl = slice(g * POOL_GROUP_DIM, (g + 1) * POOL_GROUP_DIM)
        s = s_ref[:, sl]
        for r in range(POOL_STEP_ROWS):
            s = s + jnp.where(c * POOL_STEP_ROWS + r >= POOL_CTX + 1 - w, sp_ref[r, :, sl], 0.0)
        s_ref[:, sl] = s

    for r in range(CONV_STEP_ROWS - 1):
        nsc_ref[r] = sc_ref[r + 1]
    nsc_ref[CONV_STEP_ROWS - 1] = jnp.where(last, v_ref[...], sc_next_ref[0])
    for r in range(POOL_STEP_ROWS - 1):
        nsp_ref[r] = sp_ref[r + 1]
    nsp_ref[POOL_STEP_ROWS - 1] = jnp.where(last, u_ref[...], sp_next_ref[0])

    @pl.when(last)
    def _():
        conv = acc_ref[...] + cw_ref[CONV_CTX:CONV_WIDTH, :] * v_ref[...]
        a_ref[:, 0:CONV_DIM] = _conv_post(conv, cb_ref[...], lg_ref[...], lb_ref[...]).astype(BF16)
        u = u_ref[...]
        s = s_ref[...] + u
        lane1 = lax.broadcasted_iota(jnp.int32, (1, POOL_DIM), 1)
        cnt = jnp.zeros((1, POOL_DIM), F32)
        for g, w in enumerate(POOL_WINDOWS):
            cnt = jnp.where(lane1 // POOL_GROUP_DIM == g, float(min(PAST_LEN + 1, w)), cnt)
        d_ref[...] = (s / cnt - u).astype(BF16)
        _mix_tail(a_ref, d_ref, x_ref, pw_ref, ps_ref, wo_ref, o_ref)


def _mix_out_sample(v, u, sc, sp, x, cw, cb, lg, lb, pw, ps, wo):
    nb = v.shape[0]
    whole = (v, u, x, cw, cb, lg, lb, pw, ps, wo)
    v_s, u_s, x_s, *w_s = [_resident(a.shape) for a in whole]
    rows = lambda n, index: pl.BlockSpec((n, nb, sc.shape[2]), lambda c: (index(c), 0, 0))
    conv_rows = rows(CONV_STEP_ROWS, lambda c: c)
    conv_next = rows(1, lambda c: jnp.minimum((c + 1) * CONV_STEP_ROWS, CONV_CTX - 1))
    pool_rows = rows(POOL_STEP_ROWS, lambda c: c)
    pool_next = rows(1, lambda c: jnp.minimum((c + 1) * POOL_STEP_ROWS, POOL_CTX - 1))
    return _call(
        _mix_out_sample_kernel, v, u, sc, sc, sp, sp, x, cw, cb, lg, lb, pw, ps, wo,
        name="mix_out_sample", semantics=("arbitrary",),
        out_shape=(jax.ShapeDtypeStruct((nb, D_MODEL), F32),
                   jax.ShapeDtypeStruct(sc.shape, F32), jax.ShapeDtypeStruct(sp.shape, F32)),
        grid=(SAMPLE_STEPS,),
        in_specs=[v_s, u_s, conv_rows, conv_next, pool_rows, pool_next, x_s, *w_s],
        out_specs=(pl.BlockSpec((nb, D_MODEL), lambda c: (0, 0)), conv_rows, pool_rows),
        scratch_shapes=[
            pltpu.VMEM((nb, CONV_DIM), F32),
            pltpu.VMEM((nb, POOL_DIM), F32),
            pltpu.VMEM((nb, D_MODEL), BF16),
            pltpu.VMEM((nb, POOL_DIM), BF16),
        ],
    )


def _ple_rows(x, p_ref, gp_ref, wg_ref, wp_ref, gf_ref, o_ref, n_parts=1):
    part = x.shape[0] // n_parts
    nc = 512
    for k in range(n_parts):
        rows = slice(k * part, (k + 1) * part)
        xk = x[rows]
        r = _rms(xk, gp_ref[...]).astype(BF16)
        p = p_ref[rows, :].astype(BF16)
        cols = []
        for c in range(D_MODEL // nc):
            sl = slice(c * nc, (c + 1) * nc)
            gate = jax.nn.sigmoid(jnp.dot(r, wg_ref[:, sl], preferred_element_type=F32))
            proj = jnp.dot(p, wp_ref[:, sl], preferred_element_type=F32)
            cols.append(xk[:, sl] + gate * proj)
        o_ref[rows, :] = _rms(jnp.concatenate(cols, axis=1), gf_ref[...])


def _ple_kernel(xh_ref, xr_ref, p_ref, xs_ref, psm_ref, gp_ref, wg_ref, wp_ref, gf_ref, o_ref, os_ref):
    _ple_rows(_row_tile(pl.program_id(0), xh_ref, xr_ref), p_ref, gp_ref, wg_ref, wp_ref, gf_ref, o_ref,
              n_parts=PLE_PARTS)

    @pl.when(pl.program_id(0) == pl.num_programs(0) - 1)
    def _():
        _ple_rows(xs_ref[...], psm_ref, gp_ref, wg_ref, wp_ref, gf_ref, os_ref)


def _ple(x, p, xs, ps, gp, wg, wp, gf, *, tm):
    rows, ns = p.shape[0], xs.shape[0]
    return _call(
        _ple_kernel, *x, p, xs, ps, gp, wg, wp, gf,
        name="ple", semantics=("arbitrary",),
        out_shape=(jax.ShapeDtypeStruct((rows, D_MODEL), F32), jax.ShapeDtypeStruct((ns, D_MODEL), F32)),
        grid=(rows // tm,),
        in_specs=[
            *_row_tile_specs(x, tm, lambda i: i),
            pl.BlockSpec((tm, PLE_DIM), lambda i: (i, 0)),
            _resident(xs.shape), _resident(ps.shape),
            _resident((1, D_MODEL)), _resident(wg.shape), _resident(wp.shape), _resident((1, D_MODEL)),
        ],
        out_specs=(pl.BlockSpec((tm, D_MODEL), lambda i: (i, 0)),
                   pl.BlockSpec((ns, D_MODEL), lambda i: (0, 0))),
    )


def kernel(x_prompt, x_sample, state_conv, state_pool, p_prompt, p_sample, norm_ffn1, w_ffn1_in, w_ffn1_out, norm_mix, w_in, conv_w, conv_b, conv_ln_g, conv_ln_b, pool_w, pool_scale, w_out, norm_ffn2, w_ffn2_in, w_ffn2_out, norm_ple, w_ple_gate, w_ple_proj, norm_final):
    assert norm_ffn1.shape[0] == 1, "the final norm is fused into the layer's last stage: one layer only"
    batch, seq, _ = x_prompt.shape
    nb = x_sample.shape[0]
    xp = x_prompt.reshape(batch * seq, D_MODEL)
    xs = x_sample.reshape(nb, D_MODEL)
    row = lambda a: a.reshape(1, -1)

    g1, gm, g2, gp, gf = row(norm_ffn1[0]), row(norm_mix[0]), row(norm_ffn2[0]), row(norm_ple[0]), row(norm_final)
    pool_w2d = pool_w[0].reshape(len(POOL_WINDOWS) * POOL_GROUP_DIM, POOL_GROUP_DIM)

    x1, x1s, (wi, wo, pw) = _ffn(xp, xs, g1, w_ffn1_in[0], w_ffn1_out[0], tm=FFN_TM, tf=FFN_TF,
                                 side=(w_in[0], w_out[0], pool_w2d))
    pw = pw.reshape(pool_w[0].shape)
    mix = (conv_w[0], row(conv_b[0]), row(conv_ln_g[0]), row(conv_ln_b[0]), pw, row(pool_scale[0]), wo)
    a, v_tail, u_tail, vs, us = _mix_prompt(x1, x1s, gm, wi, *mix[:-1], batch=batch, seq=seq, tm=PROMPT_TM)
    x2 = _out_proj(x1, a, wo, tm=PROMPT_TM)
    to_ctx_major = lambda st: jnp.transpose(st, (1, 0, 2))
    x2s, conv_t, pool_t = _mix_out_sample(vs, us, to_ctx_major(state_conv[0]), to_ctx_major(state_pool[0]), x1s, *mix)
    new_conv_s, new_pool_s = jnp.transpose(conv_t, (1, 0, 2)), jnp.transpose(pool_t, (1, 0, 2))
    x3, x3s, (wg, wp) = _ffn(x2, x2s, g2, w_ffn2_in[0], w_ffn2_out[0], tm=FFN_TM, tf=FFN_TF,
                             side=(w_ple_gate[0], w_ple_proj[0]))
    yp, ys = _ple(x3, p_prompt[0].reshape(batch * seq, PLE_DIM), x3s, p_sample[0].reshape(nb, PLE_DIM),
                  gp, wg, wp, gf, tm=PROMPT_TM)
    new_conv_p = v_tail[:, CONV_HALO - CONV_CTX:]
    new_pool_p = u_tail[:, POOL_HALO - POOL_CTX:]

    return (yp.reshape(batch, seq, D_MODEL), ys.reshape(nb, 1, D_MODEL),
            new_conv_p[None], new_conv_s[None], new_pool_p[None], new_pool_s[None])
```
